```python
import functools
import jax, jax.numpy as jnp
from jax import lax
import numpy as np

D_MODEL = 1024
BATCH = 8
SEQ = 8192
DEPTH = 4
DEC_BATCH = 8
DEC_SEQ = 64
PAST_LEN = 2048

CHUNK = 64
N_MIXERS = 4
EPS = 1e-6
HG_HEADS = 8
HG_DK = 128
HG_DV = D_MODEL // HG_HEADS
RET_HEADS = 4
RET_DK = D_MODEL // RET_HEADS
RET_DV = 2 * D_MODEL // RET_HEADS
ROPE_BASE = 10000.0
CONV_WIDTH = 3
FOX_HEADS = 16
FOX_DH = D_MODEL // FOX_HEADS
FOX_BIAS_INIT = 2.0
Q_BLOCK = 128
D_FF = ((8 * D_MODEL + 3 * 256 - 1) // (3 * 256)) * 256

kernel_name = 'hybrid_streaming_encoder_step'

F32 = jnp.float32


def _rmsnorm(x, g):
    xf = x.astype(F32)
    y = xf * lax.rsqrt(jnp.mean(xf * xf, axis=-1, keepdims=True) + EPS) * g.astype(F32)
    return y.astype(x.dtype)


def _swiglu(x, w_gu, w_down):
    gate, up = jnp.split(x @ w_gu, 2, axis=-1)
    return (jax.nn.silu(gate) * up) @ w_down


def _rope(x, pos):
    d = x.shape[-1]
    inv = jnp.power(ROPE_BASE, -jnp.arange(0, d, 2, dtype=F32) / d)
    ang = pos[:, None] * inv[None, :]
    cos = jnp.cos(ang)[None, :, None, :]
    sin = jnp.sin(ang)[None, :, None, :]
    x1, x2 = jnp.split(x.astype(F32), 2, axis=-1)
    return jnp.concatenate([x1 * cos - x2 * sin, x1 * sin + x2 * cos], axis=-1)


def _chunked_scan(step, state, seqs, chunk):
    T = seqs[0].shape[2]
    n = T // chunk

    def to_chunks(a):
        a = a.reshape(a.shape[0], a.shape[1], n, chunk, a.shape[-1])
        return jnp.moveaxis(a, 2, 0)

    state, out = lax.scan(lambda s, xs: step(s, *xs), state, tuple(to_chunks(a) for a in seqs))
    out = jnp.moveaxis(out, 0, 2)
    return out.reshape(out.shape[0], out.shape[1], T, out.shape[-1]), state


def _gla_chunk(state, q, k, v, logf):
    L = q.shape[2]
    G = jnp.cumsum(logf, axis=2)
    g_mid = G[:, :, L // 2][:, :, None]
    g_last = G[:, :, L - 1][:, :, None]
    scores = jnp.einsum('bhtd,bhsd->bhts', q * jnp.exp(G - g_mid), k * jnp.exp(g_mid - G))
    scores = jnp.where(jnp.tril(jnp.ones((L, L), dtype=bool)), scores, 0.0)
    o = jnp.einsum('bhts,bhsv->bhtv', scores, v) + jnp.einsum('bhtd,bhdv->bhtv', q * jnp.exp(G), state)
    new_state = jnp.swapaxes(jnp.exp(g_last), 2, 3) * state + jnp.einsum(
        'bhsd,bhsv->bhdv', k * jnp.exp(g_last - G), v)
    return new_state, o


def _ret_chunk(state, q, k, v, log_gamma):
    L = q.shape[2]
    idx = jnp.arange(L, dtype=F32)
    lg = log_gamma[:, None]
    diff = idx[:, None] - idx[None, :]
    decay = jnp.where(diff >= 0, jnp.exp(lg[:, :, None] * jnp.maximum(diff, 0.0)), 0.0)
    scores = jnp.einsum('bhtd,bhsd->bhts', q, k) * decay
    o = jnp.einsum('bhts,bhsv->bhtv', scores, v) + jnp.einsum(
        'bhtd,bhdv->bhtv', q, state) * jnp.exp(lg * (idx + 1.0))[None, :, :, None]
    new_state = jnp.exp(lg * L)[None, :, :, None] * state + jnp.einsum(
        'bhsd,bhsv->bhdv', k * jnp.exp(lg * (L - 1.0 - idx))[None, :, :, None], v)
    return new_state, o


def _hgrn2(h, layer, state, w_in, lower_bounds, out_gain, w_out):
    B, T, _ = h.shape
    nk = HG_HEADS * HG_DK
    q, f, i, g = jnp.split(h @ w_in, [nk, 2 * nk, 2 * nk + HG_HEADS * HG_DV], axis=-1)
    lb = jnp.cumsum(jax.nn.softmax(lower_bounds.astype(F32), axis=0), axis=0)[layer]
    fg = lb + (1.0 - lb) * jax.nn.sigmoid(f.astype(F32))
    k = 1.0 - fg
    logf = jnp.log(fg)

    def heads(a, d):
        return a.astype(F32).reshape(B, T, HG_HEADS, d).transpose(0, 2, 1, 3)

    o, state = _chunked_scan(_gla_chunk, state,
                             (heads(q, HG_DK), heads(k, HG_DK), heads(i, HG_DV), heads(logf, HG_DK)),
                             min(CHUNK, T))
    o = _rmsnorm(o.transpose(0, 2, 1, 3), out_gain).reshape(B, T, HG_HEADS * HG_DV)
    return (o * jax.nn.silu(g.astype(F32))).astype(h.dtype) @ w_out, state


def _retention(h, pos0, state, w_in, gn_gain, w_out):
    B, T, _ = h.shape
    nq = RET_HEADS * RET_DK
    nv = RET_HEADS * RET_DV
    q, k, v, g = jnp.split(h @ w_in, [nq, 2 * nq, 2 * nq + nv], axis=-1)
    pos = pos0 + jnp.arange(T, dtype=F32)
    q = _rope(q.reshape(B, T, RET_HEADS, RET_DK), pos)
    k = _rope(k.reshape(B, T, RET_HEADS, RET_DK), pos) * (RET_DK ** -0.5)
    v = v.astype(F32).reshape(B, T, RET_HEADS, RET_DV)
    log_gamma = jnp.log(1.0 - jnp.power(2.0, -5.0 - jnp.arange(RET_HEADS, dtype=F32)))
    tr = lambda a: a.transpose(0, 2, 1, 3)
    o, state = _chunked_scan(functools.partial(_ret_chunk, log_gamma=log_gamma), state,
                             (tr(q), tr(k), tr(v)), min(CHUNK, T))
    o = tr(o)
    mu = jnp.mean(o, axis=-1, keepdims=True)
    var = jnp.mean(jnp.square(o - mu), axis=-1, keepdims=True)
    o = (o - mu) * lax.rsqrt(var + EPS) * gn_gain.astype(F32).reshape(RET_HEADS, RET_DV)
    o = o.reshape(B, T, nv) * jax.nn.silu(g.astype(F32))
    return o.astype(h.dtype) @ w_out, state


def _shortconv(h, buf, w_in, conv_w, w_out):
    T = h.shape[1]
    b, c, u = jnp.split(h @ w_in, 3, axis=-1)
    z = c * u
    zp = jnp.concatenate([buf.astype(z.dtype), z], axis=1)
    y = conv_w[0] * zp[:, :T]
    for j in range(1, CONV_WIDTH):
        y = y + conv_w[j] * zp[:, j:j + T]
    return (b * y) @ w_out, zp[:, T:]


def _fox(h, k_cache, v_cache, lf_cache, w_in, b_f, q_gain, k_gain, w_out):
    B, T, D = h.shape
    q, k, v, fl = jnp.split(h @ w_in, [D, 2 * D, 3 * D], axis=-1)
    q = _rmsnorm(q.reshape(B, T, FOX_HEADS, FOX_DH), q_gain)
    k = _rmsnorm(k.reshape(B, T, FOX_HEADS, FOX_DH), k_gain)
    v = v.reshape(B, T, FOX_HEADS, FOX_DH)
    logf = jax.nn.log_sigmoid((fl + b_f).astype(F32))
    P = k_cache.shape[1]
    k_all = jnp.concatenate([k_cache.astype(k.dtype), k], axis=1)
    v_all = jnp.concatenate([v_cache.astype(v.dtype), v], axis=1)
    F = jnp.cumsum(jnp.concatenate([lf_cache.astype(F32), logf], axis=1), axis=1).transpose(0, 2, 1)
    qb = min(Q_BLOCK, T)
    key_pos = jnp.arange(P + T)
    scale = FOX_DH ** -0.5

    def block(bi):
        start = bi * qb
        qi = lax.dynamic_slice_in_dim(q, start, qb, axis=1)
        Fi = lax.dynamic_slice_in_dim(F, P + start, qb, axis=2)
        s = jnp.einsum('bqhd,bkhd->bhqk', qi, k_all).astype(F32) * scale
        s = s + Fi[..., None] - F[:, :, None, :]
        mask = key_pos[None, :] <= (P + start + jnp.arange(qb))[:, None]
        p = jax.nn.softmax(jnp.where(mask, s, -jnp.inf), axis=-1).astype(v_all.dtype)
        return jnp.einsum('bhqk,bkhd->bqhd', p, v_all)

    o = lax.map(block, jnp.arange(T // qb))
    o = jnp.moveaxis(o, 0, 1).reshape(B, T, D)
    return o @ w_out, k, v, logf


def _trunk(x, pos0, hg_s, ret_s, conv_s, fox_k, fox_v, fox_lf,
           norm_mix, norm_ffn, hg_w_in, hg_lower_bounds, hg_out_gain, hg_w_out,
           ret_w_in, ret_gn_gain, ret_w_out, conv_w_in, conv_w, conv_w_out,
           fox_w_in, fox_b_f, fox_q_gain, fox_k_gain, fox_w_out, ffn_w_gu, ffn_w_down):
    for layer in range(DEPTH):
        kind = layer % N_MIXERS
        h = _rmsnorm(x, norm_mix[layer])
        if kind == 0:
            mix, hg_s = _hgrn2(h, layer, hg_s.astype(F32), hg_w_in, hg_lower_bounds, hg_out_gain, hg_w_out)
        elif kind == 1:
            mix, ret_s = _retention(h, pos0, ret_s.astype(F32), ret_w_in, ret_gn_gain, ret_w_out)
        elif kind == 2:
            mix, conv_s = _shortconv(h, conv_s, conv_w_in, conv_w, conv_w_out)
        else:
            mix, fox_k, fox_v, fox_lf = _fox(h, fox_k, fox_v, fox_lf, fox_w_in, fox_b_f,
                                             fox_q_gain, fox_k_gain, fox_w_out)
        x = x + mix.astype(x.dtype)
        x = x + _swiglu(_rmsnorm(x, norm_ffn[layer]), ffn_w_gu[layer], ffn_w_down[layer])
    return x, hg_s, ret_s, conv_s, fox_k, fox_v, fox_lf


def setup_inputs(seed: int = 0) -> dict:
    key = jax.random.key(seed)
    ks = jax.random.split(key, 32)
    D = D_MODEL

    def nrm(k, shape, scale):
        return scale * jax.random.normal(k, shape, F32)

    hg_in = 2 * HG_HEADS * HG_DK + 2 * HG_HEADS * HG_DV
    ret_in = 2 * RET_HEADS * RET_DK + 2 * RET_HEADS * RET_DV
    return {
        'x_prompt': nrm(ks[0], (BATCH, SEQ, D), 1.0),
        'x_sample': nrm(ks[1], (DEC_BATCH, DEC_SEQ, D), 1.0),
        'state_hgrn': nrm(ks[2], (DEC_BATCH, HG_HEADS, HG_DK, HG_DV), 0.5),
        'state_ret': nrm(ks[3], (DEC_BATCH, RET_HEADS, RET_DK, RET_DV), 0.3),
        'state_conv': nrm(ks[4], (DEC_BATCH, CONV_WIDTH - 1, D), 1.0),
        'cache_fox_k': nrm(ks[5], (DEC_BATCH, PAST_LEN, FOX_HEADS, FOX_DH), 1.0),
        'cache_fox_v': nrm(ks[6], (DEC_BATCH, PAST_LEN, FOX_HEADS, FOX_DH), 1.0),
        'cache_fox_logf': jax.nn.log_sigmoid(FOX_BIAS_INIT + nrm(ks[7], (DEC_BATCH, PAST_LEN, FOX_HEADS), 1.0)),
        'norm_mix': 1.0 + nrm(ks[8], (DEPTH, D), 0.02),
        'norm_ffn': 1.0 + nrm(ks[9], (DEPTH, D), 0.02),
        'hg_w_in': nrm(ks[10], (D, hg_in), D ** -0.5),
        'hg_lower_bounds': nrm(ks[11], (DEPTH + 1, HG_HEADS * HG_DK), 0.1),
        'hg_out_gain': 1.0 + nrm(ks[12], (HG_DV,), 0.02),
        'hg_w_out': nrm(ks[13], (HG_HEADS * HG_DV, D), (HG_HEADS * HG_DV) ** -0.5),
        'ret_w_in': nrm(ks[14], (D, ret_in), D ** -0.5),
        'ret_gn_gain': 1.0 + nrm(ks[15], (RET_HEADS * RET_DV,), 0.02),
        'ret_w_out': nrm(ks[16], (RET_HEADS * RET_DV, D), (RET_HEADS * RET_DV) ** -0.5),
        'conv_w_in': nrm(ks[17], (D, 3 * D), D ** -0.5),
        'conv_w': nrm(ks[18], (CONV_WIDTH, D), CONV_WIDTH ** -0.5),
        'conv_w_out': nrm(ks[19], (D, D), D ** -0.5),
        'fox_w_in': nrm(ks[20], (D, 3 * D + FOX_HEADS), D ** -0.5),
        'fox_b_f': FOX_BIAS_INIT + nrm(ks[21], (FOX_HEADS,), 0.1),
        'fox_q_gain': 1.0 + nrm(ks[22], (FOX_DH,), 0.02),
        'fox_k_gain': 1.0 + nrm(ks[23], (FOX_DH,), 0.02),
        'fox_w_out': nrm(ks[24], (D, D), D ** -0.5),
        'ffn_w_gu': nrm(ks[25], (DEPTH, D, 2 * D_FF), D ** -0.5),
        'ffn_w_down': nrm(ks[26], (DEPTH, D_FF, D), D_FF ** -0.5),
    }


def reference(x_prompt, x_sample, state_hgrn, state_ret, state_conv, cache_fox_k, cache_fox_v, cache_fox_logf,
              norm_mix, norm_ffn, hg_w_in, hg_lower_bounds, hg_out_gain, hg_w_out,
              ret_w_in, ret_gn_gain, ret_w_out, conv_w_in, conv_w, conv_w_out,
              fox_w_in, fox_b_f, fox_q_gain, fox_k_gain, fox_w_out, ffn_w_gu, ffn_w_down):
    weights = (norm_mix, norm_ffn, hg_w_in, hg_lower_bounds, hg_out_gain, hg_w_out,
               ret_w_in, ret_gn_gain, ret_w_out, conv_w_in, conv_w, conv_w_out,
               fox_w_in, fox_b_f, fox_q_gain, fox_k_gain, fox_w_out, ffn_w_gu, ffn_w_down)
    B = x_prompt.shape[0]
    dt = x_prompt.dtype
    (y_prompt, hg_p, ret_p, conv_p, fk_p, fv_p, flf_p) = _trunk(
        x_prompt, 0,
        jnp.zeros((B, HG_HEADS, HG_DK, HG_DV), F32),
        jnp.zeros((B, RET_HEADS, RET_DK, RET_DV), F32),
        jnp.zeros((B, CONV_WIDTH - 1, D_MODEL), dt),
        jnp.zeros((B, 0, FOX_HEADS, FOX_DH), dt),
        jnp.zeros((B, 0, FOX_HEADS, FOX_DH), dt),
        jnp.zeros((B, 0, FOX_HEADS), F32),
        *weights)
    (y_sample, hg_s, ret_s, conv_s, fk_s, fv_s, flf_s) = _trunk(
        x_sample, PAST_LEN, state_hgrn, state_ret, state_conv,
        cache_fox_k, cache_fox_v, cache_fox_logf, *weights)
    return (y_prompt, y_sample, hg_p, hg_s, ret_p, ret_s, conv_p, conv_s,
            fk_p, fv_p, flf_p, fk_s, fv_s, flf_s)
```

```python
import functools
import math

import numpy as np
import jax
import jax.numpy as jnp
from jax import lax
from jax.experimental import pallas as pl
from jax.experimental.pallas import tpu as pltpu

F32 = jnp.float32
BF16 = jnp.bfloat16

D_MODEL = 1024
EPS = 1e-6
HG_HEADS, HG_DK, HG_DV = 8, 128, 128
HG_CHUNK = 64
RET_HEADS, RET_DK, RET_DV = 4, 256, 512
ROPE_BASE = 10000.0
CONV_WIDTH = 3
FOX_HEADS, FOX_DH = 16, 64
D_FF = 2816
LANES = 128

VMEM_LIMIT = 56 * 1024 * 1024


def _cparams(*sem):
    return pltpu.CompilerParams(dimension_semantics=sem, vmem_limit_bytes=VMEM_LIMIT)


def _resident(shape):
    nd = len(shape)
    return pl.BlockSpec(shape, lambda *_: (0,) * nd, pipeline_mode=pl.Buffered(1))


def _dot(a, b):
    return jnp.dot(a, b, preferred_element_type=F32)


def _dot_nt(a, b):
    return lax.dot_general(a, b, (((1,), (1,)), ((), ())), preferred_element_type=F32)


def _dot_tn(a, b):
    return lax.dot_general(a, b, (((0,), (0,)), ((), ())), preferred_element_type=F32)


def _sigmoid(x):
    return 1.0 / (1.0 + jnp.exp(-x))


def _rms(x, g):
    return x * lax.rsqrt(jnp.mean(x * x, axis=-1, keepdims=True) + EPS) * g


def _split3(x):
    hi = x.astype(BF16)
    r = x - hi.astype(F32)
    mid = r.astype(BF16)
    lo = (r - mid.astype(F32)).astype(BF16)
    return hi, mid, lo


def _tri(n):
    r = lax.broadcasted_iota(jnp.int32, (n, n), 0)
    c = lax.broadcasted_iota(jnp.int32, (n, n), 1)
    return r >= c


def _cumsum_rows(tri_bf16, x):
    hi, mid, lo = _split3(x)
    return _dot(tri_bf16, hi) + _dot(tri_bf16, mid) + _dot(tri_bf16, lo)


def _norm_proj_kernel(x_ref, g_ref, w_ref, o_ref, *, tn):
    h = _rms(x_ref[...], g_ref[...]).astype(BF16)
    for c in range(o_ref.shape[1] // tn):
        o_ref[:, c * tn:(c + 1) * tn] = _dot(h, w_ref[:, c * tn:(c + 1) * tn])


def _norm_proj(x, g, w, tm, tn):
    m, d = x.shape
    n = w.shape[1]
    return pl.pallas_call(
        functools.partial(_norm_proj_kernel, tn=tn),
        grid=(m // tm,),
        in_specs=[pl.BlockSpec((tm, d), lambda i: (i, 0)), _resident((1, d)), _resident((d, n))],
        out_specs=pl.BlockSpec((tm, n), lambda i: (i, 0)),
        out_shape=jax.ShapeDtypeStruct((m, n), F32),
        compiler_params=_cparams("parallel"),
        name="norm_proj",
    )(x, g.reshape(1, d), w)


def _ffn_kernel(x_ref, a_ref, wo_ref, g_ref, wg_ref, wu_ref, wd_ref, y_ref, *, tf):
    x1 = x_ref[...] + _dot(a_ref[...], wo_ref[...])
    h = _rms(x1, g_ref[...]).astype(BF16)
    acc = x1
    for c in range(D_FF // tf):
        gate = _dot(h, wg_ref[:, c * tf:(c + 1) * tf])
        up = _dot(h, wu_ref[:, c * tf:(c + 1) * tf])
        act = (gate * _sigmoid(gate) * up).astype(BF16)
        acc = acc + _dot(act, wd_ref[c * tf:(c + 1) * tf, :])
    y_ref[...] = acc


def _outproj_ffn(x, a, w_out, g, w_g, w_u, w_d, tm, tf):
    m, d = x.shape
    ka = a.shape[1]
    return pl.pallas_call(
        functools.partial(_ffn_kernel, tf=tf),
        grid=(m // tm,),
        in_specs=[pl.BlockSpec((tm, d), lambda i: (i, 0)),
                  pl.BlockSpec((tm, ka), lambda i: (i, 0)),
                  _resident((ka, d)), _resident((1, d)),
                  _resident((d, D_FF)), _resident((d, D_FF)), _resident((D_FF, d))],
        out_specs=pl.BlockSpec((tm, d), lambda i: (i, 0)),
        out_shape=jax.ShapeDtypeStruct((m, d), F32),
        compiler_params=_cparams("parallel"),
        name="outproj_ffn",
    )(x, a, w_out, g.reshape(1, d), w_g, w_u, w_d)


def _hgrn_kernel(p_ref, lbw_ref, gain_ref, s0_ref, o_ref, sout_ref, st_ref, *, layer, nchunk):
    t = pl.program_id(1)
    L = HG_CHUNK
    nk = HG_HEADS * HG_DK

    @pl.when(t == 0)
    def _():
        for h in range(HG_HEADS):
            st_ref[h] = s0_ref[0, h].T

    lbw = lbw_ref[...]
    e = jnp.exp(lbw - jnp.max(lbw, axis=0, keepdims=True))
    lb = jnp.sum(e[:layer + 1], axis=0, keepdims=True) / jnp.sum(e, axis=0, keepdims=True)
    tri = _tri(L)
    tri_b = tri.astype(BF16)
    gain = gain_ref[...]

    def chunk(c, carry):
        rows = pl.ds(pl.multiple_of(c * L, L), L)
        q = p_ref[rows, 0:nk]
        f = p_ref[rows, nk:2 * nk]
        fg = lb + (1.0 - lb) * _sigmoid(f)
        k = 1.0 - fg
        G = _cumsum_rows(tri_b, jnp.log(fg))
        g_mid = G[L // 2:L // 2 + 1]
        g_last = G[L - 1:L]
        qa = (q * jnp.exp(G - g_mid)).astype(BF16)
        kb = (k * jnp.exp(g_mid - G)).astype(BF16)
        qs = (q * jnp.exp(G)).astype(BF16)
        kl = (k * jnp.exp(g_last - G)).astype(BF16)
        dec = jnp.exp(g_last)
        for h in range(HG_HEADS):
            sl = slice(h * HG_DK, (h + 1) * HG_DK)
            v = p_ref[rows, 2 * nk + h * HG_DV:2 * nk + (h + 1) * HG_DV].astype(BF16)
            sc = jnp.where(tri, _dot_nt(qa[:, sl], kb[:, sl]), 0.0).astype(BF16)
            st = st_ref[h]
            o = _dot(sc, v) + _dot_nt(qs[:, sl], st.astype(BF16))
            st_ref[h] = st * dec[:, sl] + _dot_tn(v, kl[:, sl])
            gate = p_ref[rows, 3 * nk + h * HG_DV:3 * nk + (h + 1) * HG_DV]
            o_ref[rows, h * HG_DV:(h + 1) * HG_DV] = (
                _rms(o, gain) * (gate * _sigmoid(gate))).astype(BF16)
        return carry

    lax.fori_loop(0, nchunk, chunk, 0)

    @pl.when(t == pl.num_programs(1) - 1)
    def _():
        for h in range(HG_HEADS):
            sout_ref[0, h] = st_ref[h].T


def _hgrn_core(proj, lower_bounds, out_gain, state, layer, bsz, seq, tm):
    nt = seq // tm
    w = proj.shape[1]
    st_shape = (1, HG_HEADS, HG_DK, HG_DV)
    return pl.pallas_call(
        functools.partial(_hgrn_kernel, layer=layer, nchunk=tm // HG_CHUNK),
        grid=(bsz, nt),
        in_specs=[pl.BlockSpec((tm, w), lambda b, t: (b * nt + t, 0)),
                  _resident(lower_bounds.shape), _resident((1, HG_DV)),
                  pl.BlockSpec(st_shape, lambda b, t: (b, 0, 0, 0))],
        out_specs=[pl.BlockSpec((tm, HG_HEADS * HG_DV), lambda b, t: (b * nt + t, 0)),
                   pl.BlockSpec(st_shape, lambda b, t: (b, 0, 0, 0))],
        out_shape=[jax.ShapeDtypeStruct((bsz * seq, HG_HEADS * HG_DV), BF16),
                   jax.ShapeDtypeStruct((bsz,) + st_shape[1:], F32)],
        scratch_shapes=[pltpu.VMEM((HG_HEADS, HG_DV, HG_DK), F32)],
        compiler_params=_cparams("parallel", "arbitrary"),
        name="hgrn_core",
    )(proj, lower_bounds, out_gain.reshape(1, HG_DV), state)


def _rope_table_kernel(inv_ref, cos_ref, sin_ref, *, pos0):
    tb = cos_ref.shape[0]
    pos = (pos0 + pl.program_id(0) * tb
           + lax.broadcasted_iota(jnp.int32, cos_ref.shape, 0)).astype(F32)
    ang = pos * inv_ref[...]
    cos_ref[...] = jnp.cos(ang)
    sin_ref[...] = jnp.sin(ang)


def _rope_table(inv, pos0, seq, tb):
    half = inv.shape[0]
    spec = pl.BlockSpec((tb, half), lambda i: (i, 0))
    return pl.pallas_call(
        functools.partial(_rope_table_kernel, pos0=pos0),
        grid=(seq // tb,),
        in_specs=[_resident((1, half))],
        out_specs=[spec, spec],
        out_shape=[jax.ShapeDtypeStruct((seq, half), F32)] * 2,
        compiler_params=_cparams("parallel"),
        name="rope_table",
    )(inv.reshape(1, half))


def _ret_kernel(p_ref, cos_ref, sin_ref, gn_ref, s0_ref, o_ref, sout_ref, st_ref, *, L, nchunk):
    t = pl.program_id(1)
    nq = RET_HEADS * RET_DK
    nv = RET_HEADS * RET_DV
    half = RET_DK // 2

    @pl.when(t == 0)
    def _():
        st_ref[...] = s0_ref[0]

    ti = lax.broadcasted_iota(jnp.int32, (L, 1), 0).astype(F32)
    diff = (lax.broadcasted_iota(jnp.int32, (L, L), 0)
            - lax.broadcasted_iota(jnp.int32, (L, L), 1)).astype(F32)
    scale = RET_DK ** -0.5

    def chunk(c, carry):
        rows = pl.ds(pl.multiple_of(c * L, L), L)
        cos = cos_ref[rows, :]
        sin = sin_ref[rows, :]
        for h in range(RET_HEADS):
            lg = math.log(1.0 - 2.0 ** (-5.0 - h))
            q1 = p_ref[rows, h * RET_DK:h * RET_DK + half]
            q2 = p_ref[rows, h * RET_DK + half:(h + 1) * RET_DK]
            k1 = p_ref[rows, nq + h * RET_DK:nq + h * RET_DK + half]
            k2 = p_ref[rows, nq + h * RET_DK + half:nq + (h + 1) * RET_DK]
            qr = jnp.concatenate([q1 * cos - q2 * sin, q1 * sin + q2 * cos], axis=1).astype(BF16)
            kr = jnp.concatenate([k1 * cos - k2 * sin, k1 * sin + k2 * cos], axis=1) * scale
            v = p_ref[rows, 2 * nq + h * RET_DV:2 * nq + (h + 1) * RET_DV].astype(BF16)
            decay = jnp.where(diff >= 0, jnp.exp(lg * jnp.maximum(diff, 0.0)), 0.0)
            sc = (_dot_nt(qr, kr.astype(BF16)) * decay).astype(BF16)
            st = st_ref[h]
            o = _dot(sc, v) + _dot(qr, st.astype(BF16)) * jnp.exp(lg * (ti + 1.0))
            kd = (kr * jnp.exp(lg * (L - 1.0 - ti))).astype(BF16)
            st_ref[h] = math.exp(lg * L) * st + _dot_tn(kd, v)
            mu = jnp.mean(o, axis=-1, keepdims=True)
            d = o - mu
            var = jnp.mean(d * d, axis=-1, keepdims=True)
            on = d * lax.rsqrt(var + EPS) * gn_ref[:, h * RET_DV:(h + 1) * RET_DV]
            gate = p_ref[rows, 2 * nq + nv + h * RET_DV:2 * nq + nv + (h + 1) * RET_DV]
            o_ref[rows, h * RET_DV:(h + 1) * RET_DV] = (on * (gate * _sigmoid(gate))).astype(BF16)
        return carry

    lax.fori_loop(0, nchunk, chunk, 0)

    @pl.when(t == pl.num_programs(1) - 1)
    def _():
        sout_ref[0] = st_ref[...]


def _ret_core(proj, cos, sin, gn_gain, state, bsz, seq, tm, chunk):
    nt = seq // tm
    w = proj.shape[1]
    nv = RET_HEADS * RET_DV
    half = RET_DK // 2
    st_shape = (1, RET_HEADS, RET_DK, RET_DV)
    return pl.pallas_call(
        functools.partial(_ret_kernel, L=chunk, nchunk=tm // chunk),
        grid=(bsz, nt),
        in_specs=[pl.BlockSpec((tm, w), lambda b, t: (b * nt + t, 0)),
                  pl.BlockSpec((tm, half), lambda b, t: (t, 0)),
                  pl.BlockSpec((tm, half), lambda b, t: (t, 0)),
                  _resident((1, nv)),
                  pl.BlockSpec(st_shape, lambda b, t: (b, 0, 0, 0))],
        out_specs=[pl.BlockSpec((tm, nv), lambda b, t: (b * nt + t, 0)),
                   pl.BlockSpec(st_shape, lambda b, t: (b, 0, 0, 0))],
        out_shape=[jax.ShapeDtypeStruct((bsz * seq, nv), BF16),
                   jax.ShapeDtypeStruct((bsz,) + st_shape[1:], F32)],
        scratch_shapes=[pltpu.VMEM((RET_HEADS, RET_DK, RET_DV), F32)],
        compiler_params=_cparams("parallel", "arbitrary"),
        name="ret_core",
    )(proj, cos, sin, gn_gain.reshape(1, nv), state)


_CONV_PAD = 8


def _conv_kernel(x_ref, g_ref, w_ref, cw_ref, s0_ref, a_ref, sout_ref, z_ref):
    t = pl.program_id(1)
    tm, d = x_ref.shape
    nc = CONV_WIDTH - 1

    @pl.when(t == 0)
    def _():
        z_ref[_CONV_PAD - nc:_CONV_PAD, :] = s0_ref[0]

    h = _rms(x_ref[...], g_ref[...]).astype(BF16)
    b = _dot(h, w_ref[:, 0:d])
    z_ref[_CONV_PAD:_CONV_PAD + tm, :] = _dot(h, w_ref[:, d:2 * d]) * _dot(h, w_ref[:, 2 * d:3 * d])
    y = cw_ref[0:1, :] * z_ref[_CONV_PAD - nc:_CONV_PAD - nc + tm, :]
    for j in range(1, CONV_WIDTH):
        y = y + cw_ref[j:j + 1, :] * z_ref[_CONV_PAD - nc + j:_CONV_PAD - nc + j + tm, :]
    a_ref[...] = (b * y).astype(BF16)
    last = z_ref[_CONV_PAD + tm - nc:_CONV_PAD + tm, :]
    z_ref[_CONV_PAD - nc:_CONV_PAD, :] = last

    @pl.when(t == pl.num_programs(1) - 1)
    def _():
        sout_ref[0] = last


def _conv_core(x, g, w_in, conv_w, state, bsz, seq, tm):
    nt = seq // tm
    d = x.shape[1]
    nc = CONV_WIDTH - 1
    return pl.pallas_call(
        _conv_kernel,
        grid=(bsz, nt),
        in_specs=[pl.BlockSpec((tm, d), lambda b, t: (b * nt + t, 0)),
                  _resident((1, d)), _resident((d, 3 * d)), _resident((CONV_WIDTH, d)),
                  pl.BlockSpec((1, nc, d), lambda b, t: (b, 0, 0))],
        out_specs=[pl.BlockSpec((tm, d), lambda b, t: (b * nt + t, 0)),
                   pl.BlockSpec((1, nc, d), lambda b, t: (b, 0, 0))],
        out_shape=[jax.ShapeDtypeStruct((bsz * seq, d), BF16),
                   jax.ShapeDtypeStruct((bsz, nc, d), F32)],
        scratch_shapes=[pltpu.VMEM((_CONV_PAD + tm, d), F32)],
        compiler_params=_cparams("parallel", "arbitrary"),
        name="conv_core",
    )(x, g.reshape(1, d), w_in, conv_w, state)


_AUG = 6


def _fox_prep_kernel(x_ref, g_ref, w_ref, wf_ref, bf_ref, qg_ref, kg_ref, sel_ref, selt_ref,
                     qs_ref, kn_ref, kb_ref, v_ref, vb_ref, lf_ref):
    d = x_ref.shape[1]
    h = _rms(x_ref[...], g_ref[...]).astype(BF16)

    def head_norm(a, gain):
        ms = _dot((a * a).astype(BF16), sel_ref[...]) * (1.0 / FOX_DH)
        rs = lax.rsqrt(ms + EPS)
        hi = rs.astype(BF16)
        lo = (rs - hi.astype(F32)).astype(BF16)
        return a * (_dot(hi, selt_ref[...]) + _dot(lo, selt_ref[...])) * gain

    qn = head_norm(_dot(h, w_ref[:, 0:d]), qg_ref[...])
    qs_ref[...] = (qn * (FOX_DH ** -0.5)).astype(BF16)
    kn = head_norm(_dot(h, w_ref[:, d:2 * d]), kg_ref[...])
    kn_ref[...] = kn
    kb_ref[...] = kn.astype(BF16)
    v = _dot(h, w_ref[:, 2 * d:3 * d])
    v_ref[...] = v
    vb_ref[...] = v.astype(BF16)
    fl = _dot(h, wf_ref[...]) + bf_ref[...]
    ls = jnp.minimum(fl, 0.0) - jnp.log(1.0 + jnp.exp(-jnp.abs(fl)))
    lane = lax.broadcasted_iota(jnp.int32, fl.shape, 1)
    lf_ref[...] = jnp.where(lane < FOX_HEADS, ls, 0.0)


def _fox_prep(x, g, w_qkv, w_f, b_f, q_gain, k_gain, tm):
    m, d = x.shape
    col = np.arange(d) // FOX_DH
    sel = (col[:, None] == np.arange(LANES)[None, :]).astype(np.float32)
    row = pl.BlockSpec((tm, d), lambda i: (i, 0))
    return pl.pallas_call(
        _fox_prep_kernel,
        grid=(m // tm,),
        in_specs=[row, _resident((1, d)), _resident((d, 3 * d)), _resident((d, LANES)),
                  _resident((1, LANES)), _resident((1, d)), _resident((1, d)),
                  _resident((d, LANES)), _resident((LANES, d))],
        out_specs=[row, row, row, row, row, pl.BlockSpec((tm, LANES), lambda i: (i, 0))],
        out_shape=[jax.ShapeDtypeStruct((m, d), BF16), jax.ShapeDtypeStruct((m, d), F32),
                   jax.ShapeDtypeStruct((m, d), BF16), jax.ShapeDtypeStruct((m, d), F32),
                   jax.ShapeDtypeStruct((m, d), BF16), jax.ShapeDtypeStruct((m, LANES), F32)],
        compiler_params=_cparams("parallel"),
        name="fox_prep",
    )(x, g.reshape(1, d), w_qkv, w_f, b_f, jnp.tile(q_gain, FOX_HEADS).reshape(1, d),
      jnp.tile(k_gain, FOX_HEADS).reshape(1, d), jnp.asarray(sel, BF16), jnp.asarray(sel.T, BF16))


def _fox_gate_kernel(lf_ref, pq_ref, pk_ref, cq_ref, ck_ref, qa_ref, ka_ref, carry_ref):
    t = pl.program_id(1)
    cb = lf_ref.shape[1]

    @pl.when(t == 0)
    def _():
        carry_ref[...] = jnp.zeros_like(carry_ref)

    F = _cumsum_rows(_tri(cb).astype(BF16), lf_ref[0]) + carry_ref[...]
    carry_ref[...] = F[cb - 1:cb]
    qa = cq_ref[...]
    ka = ck_ref[...]
    for j, part in enumerate(_split3(F)):
        qa = qa + _dot(part, pq_ref[j])
        ka = ka + _dot(part, pk_ref[j])
    qa_ref[0] = qa.astype(BF16)
    ka_ref[0] = ka.astype(BF16)


def _fox_gate(lf_all, cb):
    bsz, tk, _ = lf_all.shape
    place_q = np.zeros((3, LANES, LANES), np.float32)
    place_k = np.zeros((3, LANES, LANES), np.float32)
    const_q = np.zeros((1, LANES), np.float32)
    const_k = np.zeros((1, LANES), np.float32)
    for h in range(FOX_HEADS):
        for j in range(3):
            place_q[j, h, h * _AUG + j] = 1.0
            place_k[j, h, h * _AUG + 3 + j] = -1.0
            const_q[0, h * _AUG + 3 + j] = 1.0
            const_k[0, h * _AUG + j] = 1.0
    spec = pl.BlockSpec((1, cb, LANES), lambda b, t: (b, t, 0))
    return pl.pallas_call(
        _fox_gate_kernel,
        grid=(bsz, tk // cb),
        in_specs=[spec, _resident(place_q.shape), _resident(place_k.shape),
                  _resident(const_q.shape), _resident(const_k.shape)],
        out_specs=[spec, spec],
        out_shape=[jax.ShapeDtypeStruct((bsz, tk, LANES), BF16)] * 2,
        scratch_shapes=[pltpu.VMEM((1, LANES), F32)],
        compiler_params=_cparams("parallel", "arbitrary"),
        name="fox_gate",
    )(lf_all, jnp.asarray(place_q, BF16), jnp.asarray(place_k, BF16),
      jnp.asarray(const_q), jnp.asarray(const_k))


def _fox_attn_kernel(q_ref, qa_ref, k_ref, ka_ref, v_ref, o_ref, *, blk, past_blocks):
    pair = pl.program_id(1)
    qi = pl.program_id(2)
    lane = lax.broadcasted_iota(jnp.int32, (blk, LANES), 1)
    low = lane < FOX_DH
    q = q_ref[0]
    qa = qa_ref[0]
    qops = []
    for s in range(2):
        lo = (2 * pair + s) * _AUG
        qh = jnp.where(low if s == 0 else ~low, q, jnp.zeros_like(q))
        qah = jnp.where((lane >= lo) & (lane < lo + _AUG), qa, jnp.zeros_like(qa))
        qops.append(jnp.concatenate([qh, qah], axis=1))
    causal = _tri(blk)
    n_full = past_blocks + qi

    def step(j, carry, masked):
        rows = pl.ds(pl.multiple_of(j * blk, blk), blk)
        kop = jnp.concatenate([k_ref[0, rows, :], ka_ref[0, rows, :]], axis=1)
        v = v_ref[0, rows, :]
        out = []
        for s in range(2):
            m, l, acc = carry[s]
            sc = _dot_nt(qops[s], kop)
            if masked:
                sc = jnp.where(causal, sc, -jnp.inf)
            m_new = jnp.maximum(m, jnp.max(sc, axis=-1, keepdims=True))
            alpha = jnp.exp(m - m_new)
            p = jnp.exp(sc - m_new)
            l = alpha * l + jnp.sum(p, axis=-1, keepdims=True)
            acc = alpha * acc + _dot(p.astype(BF16), v)
            out.append((m_new, l, acc))
        return tuple(out)

    init = tuple((jnp.full((blk, 1), -jnp.inf, F32), jnp.zeros((blk, 1), F32),
                  jnp.zeros((blk, LANES), F32)) for _ in range(2))
    carry = lax.fori_loop(0, n_full, lambda j, c: step(j, c, False), init)
    (_, l0, a0), (_, l1, a1) = step(n_full, carry, True)
    o_ref[0] = jnp.where(low, a0 / l0, a1 / l1).astype(BF16)


def _fox_attn(qs, qa, kb, ka, vb, bsz, seq, past, blk):
    tk = past + seq
    d = qs.shape[-1]
    pb = past // blk
    qspec = pl.BlockSpec((1, blk, LANES), lambda b, p, i: (b, i, p))
    kspec = pl.BlockSpec((1, tk, LANES), lambda b, p, i: (b, 0, p))
    return pl.pallas_call(
        functools.partial(_fox_attn_kernel, blk=blk, past_blocks=pb),
        grid=(bsz, d // LANES, seq // blk),
        in_specs=[qspec,
                  pl.BlockSpec((1, blk, LANES), lambda b, p, i: (b, pb + i, 0)),
                  kspec,
                  pl.BlockSpec((1, tk, LANES), lambda b, p, i: (b, 0, 0)),
                  kspec],
        out_specs=qspec,
        out_shape=jax.ShapeDtypeStruct((bsz, seq, d), BF16),
        compiler_params=_cparams("parallel", "parallel", "arbitrary"),
        name="fox_attn",
    )(qs, qa, kb, ka, vb)


def _trunk(x, pos0, hg_s, ret_s, conv_s, fox_k, fox_v, fox_lf, W):
    bsz, seq, d = x.shape
    m = bsz * seq
    past = fox_k.shape[1]
    tm = min(seq, 256)
    tm_ffn = min(m, 512)
    x = x.reshape(m, d)

    def ffn(x, a, w_out, layer):
        return _outproj_ffn(x, a, w_out, W["norm_ffn"][layer], W["ffn_w_g"][layer],
                            W["ffn_w_u"][layer], W["ffn_w_d"][layer], tm_ffn, D_FF // 2)

    proj = _norm_proj(x, W["norm_mix"][0], W["hg_w_in"], tm, 512)
    a, hg_s = _hgrn_core(proj, W["hg_lower_bounds"], W["hg_out_gain"], hg_s, 0, bsz, seq, tm)
    x = ffn(x, a, W["hg_w_out"], 0)

    proj = _norm_proj(x, W["norm_mix"][1], W["ret_w_in"], tm, 512)
    inv = jnp.power(ROPE_BASE, -jnp.arange(0, RET_DK, 2, dtype=F32) / RET_DK)
    cos, sin = _rope_table(inv, pos0, seq, tm)
    a, ret_s = _ret_core(proj, cos, sin, W["ret_gn_gain"], ret_s, bsz, seq, tm, min(seq, 128))
    x = ffn(x, a, W["ret_w_out"], 1)

    a, conv_s = _conv_core(x, W["norm_mix"][2], W["conv_w_in"], W["conv_w"], conv_s, bsz, seq, tm)
    x = ffn(x, a, W["conv_w_out"], 2)

    qs, kn, kb, v, vb, lf = _fox_prep(x, W["norm_mix"][3], W["fox_w_qkv"], W["fox_w_f"],
                                      W["fox_b_f"], W["fox_q_gain"], W["fox_k_gain"], tm)
    blk = min(seq, 512)
    lf3 = lf.reshape(bsz, seq, LANES)
    kb3 = kb.reshape(bsz, seq, d)
    vb3 = vb.reshape(bsz, seq, d)
    if past:
        lf3 = jnp.concatenate([jnp.pad(fox_lf, ((0, 0), (0, 0), (0, LANES - FOX_HEADS))), lf3], axis=1)
        kb3 = jnp.concatenate([fox_k.reshape(bsz, past, d).astype(BF16), kb3], axis=1)
        vb3 = jnp.concatenate([fox_v.reshape(bsz, past, d).astype(BF16), vb3], axis=1)
    qa, ka = _fox_gate(lf3, min(seq, 256))
    a = _fox_attn(qs.reshape(bsz, seq, d), qa, kb3, ka, vb3, bsz, seq, past, blk)
    x = ffn(x, a.reshape(m, d), W["fox_w_out"], 3)

    return (x.reshape(bsz, seq, d), hg_s, ret_s, conv_s,
            kn.reshape(bsz, seq, FOX_HEADS, FOX_DH), v.reshape(bsz, seq, FOX_HEADS, FOX_DH),
            lf[:, :FOX_HEADS].reshape(bsz, seq, FOX_HEADS))


def kernel(x_prompt, x_sample, state_hgrn, state_ret, state_conv, cache_fox_k, cache_fox_v, cache_fox_logf, norm_mix, norm_ffn, hg_w_in, hg_lower_bounds, hg_out_gain, hg_w_out, ret_w_in, ret_gn_gain, ret_w_out, conv_w_in, conv_w, conv_w_out, fox_w_in, fox_b_f, fox_q_gain, fox_k_gain, fox_w_out, ffn_w_gu, ffn_w_down):
    d = D_MODEL
    bf = lambda w: w.astype(BF16)
    W = dict(
        norm_mix=norm_mix, norm_ffn=norm_ffn,
        hg_w_in=bf(hg_w_in), hg_lower_bounds=hg_lower_bounds, hg_out_gain=hg_out_gain,
        hg_w_out=bf(hg_w_out),
        ret_w_in=bf(ret_w_in), ret_gn_gain=ret_gn_gain, ret_w_out=bf(ret_w_out),
        conv_w_in=bf(conv_w_in), conv_w=conv_w, conv_w_out=bf(conv_w_out),
        fox_w_qkv=bf(fox_w_in[:, :3 * d]),
        fox_w_f=bf(jnp.pad(fox_w_in[:, 3 * d:], ((0, 0), (0, LANES - FOX_HEADS)))),
        fox_b_f=jnp.pad(fox_b_f, (0, LANES - FOX_HEADS)).reshape(1, LANES),
        fox_q_gain=fox_q_gain, fox_k_gain=fox_k_gain, fox_w_out=bf(fox_w_out),
        ffn_w_g=bf(ffn_w_gu[:, :, :D_FF]), ffn_w_u=bf(ffn_w_gu[:, :, D_FF:]), ffn_w_d=bf(ffn_w_down),
    )
    bsz = x_prompt.shape[0]
    dt = x_prompt.dtype
    (y_p, hg_p, ret_p, conv_p, fk_p, fv_p, flf_p) = _trunk(
        x_prompt, 0,
        jnp.zeros((bsz, HG_HEADS, HG_DK, HG_DV), F32),
        jnp.zeros((bsz, RET_HEADS, RET_DK, RET_DV), F32),
        jnp.zeros((bsz, CONV_WIDTH - 1, d), dt),
        jnp.zeros((bsz, 0, FOX_HEADS, FOX_DH), dt),
        jnp.zeros((bsz, 0, FOX_HEADS, FOX_DH), dt),
        jnp.zeros((bsz, 0, FOX_HEADS), F32), W)
    past = cache_fox_k.shape[1]
    (y_s, hg_s, ret_s, conv_s, fk_s, fv_s, flf_s) = _trunk(
        x_sample, past, state_hgrn, state_ret, state_conv,
        cache_fox_k, cache_fox_v, cache_fox_logf, W)
    return (y_p, y_s, hg_p, hg_s, ret_p, ret_s, conv_p, conv_s,
            fk_p, fv_p, flf_p, fk_s, fv_s, flf_s)
```

```python
import functools
import math

import numpy as np
import jax
import jax.numpy as jnp
from jax import lax
from jax.experimental import pallas as pl
from jax.experimental.pallas import tpu as pltpu

F32 = jnp.float32
BF16 = jnp.bfloat16

D_MODEL = 1024
EPS = 1e-6
HG_HEADS, HG_DK, HG_DV = 8, 128, 128
HG_CHUNK = 64
RET_HEADS, RET_DK, RET_DV = 4, 256, 512
ROPE_BASE = 10000.0
CONV_WIDTH = 3
FOX_HEADS, FOX_DH = 16, 64
D_FF = 2816
LANES = 128

VMEM_LIMIT = 56 * 1024 * 1024


def _cparams(*sem):
    return pltpu.CompilerParams(dimension_semantics=sem, vmem_limit_bytes=VMEM_LIMIT)


def _resident(shape):
    nd = len(shape)
    return pl.BlockSpec(shape, lambda *_: (0,) * nd, pipeline_mode=pl.Buffered(1))


def _dot(a, b):
    return jnp.dot(a, b, preferred_element_type=F32)


def _dot_nt(a, b):
    return lax.dot_general(a, b, (((1,), (1,)), ((), ())), preferred_element_type=F32)


def _dot_tn(a, b):
    return lax.dot_general(a, b, (((0,), (0,)), ((), ())), preferred_element_type=F32)


def _sigmoid(x):
    return 1.0 / (1.0 + jnp.exp(-x))


def _rms(x, g):
    return x * lax.rsqrt(jnp.mean(x * x, axis=-1, keepdims=True) + EPS) * g


def _split3(x):
    hi = x.astype(BF16)
    r = x - hi.astype(F32)
    mid = r.astype(BF16)
    lo = (r - mid.astype(F32)).astype(BF16)
    return hi, mid, lo


def _tri(n):
    r = lax.broadcasted_iota(jnp.int32, (n, n), 0)
    c = lax.broadcasted_iota(jnp.int32, (n, n), 1)
    return r >= c


def _cumsum_rows(tri_bf16, x):
    hi, mid, lo = _split3(x)
    return _dot(tri_bf16, hi) + _dot(tri_bf16, mid) + _dot(tri_bf16, lo)


def _norm_proj_kernel(x_ref, g_ref, w_ref, o_ref, *, tn):
    h = _rms(x_ref[...], g_ref[...]).astype(BF16)
    for c in range(o_ref.shape[1] // tn):
        o_ref[:, c * tn:(c + 1) * tn] = _dot(h, w_ref[:, c * tn:(c + 1) * tn])


def _norm_proj(x, g, w, tm, tn):
    m, d = x.shape
    n = w.shape[1]
    return pl.pallas_call(
        functools.partial(_norm_proj_kernel, tn=tn),
        grid=(m // tm,),
        in_specs=[pl.BlockSpec((tm, d), lambda i: (i, 0)), _resident((1, d)), _resident((d, n))],
        out_specs=pl.BlockSpec((tm, n), lambda i: (i, 0)),
        out_shape=jax.ShapeDtypeStruct((m, n), F32),
        compiler_params=_cparams("parallel"),
        name="norm_proj",
    )(x, g.reshape(1, d), w)


def _ffn_kernel(x_ref, a_ref, wo_ref, g_ref, wg_ref, wu_ref, wd_ref, y_ref, *, tf):
    x1 = x_ref[...] + _dot(a_ref[...], wo_ref[...])
    h = _rms(x1, g_ref[...]).astype(BF16)
    acc = x1
    for c in range(D_FF // tf):
        gate = _dot(h, wg_ref[:, c * tf:(c + 1) * tf])
        up = _dot(h, wu_ref[:, c * tf:(c + 1) * tf])
        act = (gate * _sigmoid(gate) * up).astype(BF16)
        acc = acc + _dot(act, wd_ref[c * tf:(c + 1) * tf, :])
    y_ref[...] = acc


def _outproj_ffn(x, a, w_out, g, w_g, w_u, w_d, tm, tf):
    m, d = x.shape
    ka = a.shape[1]
    return pl.pallas_call(
        functools.partial(_ffn_kernel, tf=tf),
        grid=(m // tm,),
        in_specs=[pl.BlockSpec((tm, d), lambda i: (i, 0)),
                  pl.BlockSpec((tm, ka), lambda i: (i, 0)),
                  _resident((ka, d)), _resident((1, d)),
                  _resident((d, D_FF)), _resident((d, D_FF)), _resident((D_FF, d))],
        out_specs=pl.BlockSpec((tm, d), lambda i: (i, 0)),
        out_shape=jax.ShapeDtypeStruct((m, d), F32),
        compiler_params=_cparams("parallel"),
        name="outproj_ffn",
    )(x, a, w_out, g.reshape(1, d), w_g, w_u, w_d)


def _hgrn_kernel(p_ref, lbw_ref, gain_ref, s0_ref, o_ref, sout_ref, st_ref, *, layer, nchunk):
    t = pl.program_id(1)
    L = HG_CHUNK
    nk = HG_HEADS * HG_DK

    @pl.when(t == 0)
    def _():
        for h in range(HG_HEADS):
            st_ref[h] = s0_ref[0, h].T

    lbw = lbw_ref[...]
    e = jnp.exp(lbw - jnp.max(lbw, axis=0, keepdims=True))
    lb = jnp.sum(e[:layer + 1], axis=0, keepdims=True) / jnp.sum(e, axis=0, keepdims=True)
    gain = gain_ref[...]
    tm = p_ref.shape[0]

    r = lax.broadcasted_iota(jnp.int32, (tm, tm), 0)
    c = lax.broadcasted_iota(jnp.int32, (tm, tm), 1)
    shift = L.bit_length() - 1
    tri = (r >= c) & ((r >> shift) == (c >> shift))
    q = p_ref[:, 0:nk]
    fg = lb + (1.0 - lb) * _sigmoid(p_ref[:, nk:2 * nk])
    k = 1.0 - fg
    G = _cumsum_rows(tri.astype(BF16), jnp.log(fg))

    def per_chunk_row(offset):
        return jnp.concatenate(
            [jnp.broadcast_to(G[i * L + offset:i * L + offset + 1], (L, nk)) for i in range(nchunk)],
            axis=0)

    g_mid = per_chunk_row(L // 2)
    g_last = per_chunk_row(L - 1)
    qa = (q * jnp.exp(G - g_mid)).astype(BF16)
    kb = (k * jnp.exp(g_mid - G)).astype(BF16)
    qs = (q * jnp.exp(G)).astype(BF16)
    kl = (k * jnp.exp(g_last - G)).astype(BF16)
    for h in range(HG_HEADS):
        sl = slice(h * HG_DK, (h + 1) * HG_DK)
        v = p_ref[:, 2 * nk + h * HG_DV:2 * nk + (h + 1) * HG_DV].astype(BF16)
        sc = jnp.where(tri, _dot_nt(qa[:, sl], kb[:, sl]), 0.0).astype(BF16)
        o_intra = _dot(sc, v)
        st = st_ref[h]
        outs = []
        for i in range(nchunk):
            rows = slice(i * L, (i + 1) * L)
            outs.append(o_intra[rows] + _dot_nt(qs[rows, sl], st.astype(BF16)))
            dec = jnp.exp(G[(i + 1) * L - 1:(i + 1) * L, sl])
            st = st * dec + _dot_tn(v[rows], kl[rows, sl])
        st_ref[h] = st
        gate = p_ref[:, 3 * nk + h * HG_DV:3 * nk + (h + 1) * HG_DV]
        o_ref[:, h * HG_DV:(h + 1) * HG_DV] = (
            _rms(jnp.concatenate(outs, axis=0), gain) * (gate * _sigmoid(gate))).astype(BF16)

    @pl.when(t == pl.num_programs(1) - 1)
    def _():
        for h in range(HG_HEADS):
            sout_ref[0, h] = st_ref[h].T


def _hgrn_core(proj, lower_bounds, out_gain, state, layer, bsz, seq, tm):
    nt = seq // tm
    w = proj.shape[1]
    st_shape = (1, HG_HEADS, HG_DK, HG_DV)
    return pl.pallas_call(
        functools.partial(_hgrn_kernel, layer=layer, nchunk=tm // HG_CHUNK),
        grid=(bsz, nt),
        in_specs=[pl.BlockSpec((tm, w), lambda b, t: (b * nt + t, 0)),
                  _resident(lower_bounds.shape), _resident((1, HG_DV)),
                  pl.BlockSpec(st_shape, lambda b, t: (b, 0, 0, 0))],
        out_specs=[pl.BlockSpec((tm, HG_HEADS * HG_DV), lambda b, t: (b * nt + t, 0)),
                   pl.BlockSpec(st_shape, lambda b, t: (b, 0, 0, 0))],
        out_shape=[jax.ShapeDtypeStruct((bsz * seq, HG_HEADS * HG_DV), BF16),
                   jax.ShapeDtypeStruct((bsz,) + st_shape[1:], F32)],
        scratch_shapes=[pltpu.VMEM((HG_HEADS, HG_DV, HG_DK), F32)],
        compiler_params=_cparams("parallel", "arbitrary"),
        name="hgrn_core",
    )(proj, lower_bounds, out_gain.reshape(1, HG_DV), state)


def _rope_table_kernel(inv_ref, cos_ref, sin_ref, *, pos0):
    tb = cos_ref.shape[0]
    pos = (pos0 + pl.program_id(0) * tb
           + lax.broadcasted_iota(jnp.int32, cos_ref.shape, 0)).astype(F32)
    ang = pos * inv_ref[...]
    cos_ref[...] = jnp.cos(ang)
    sin_ref[...] = jnp.sin(ang)


def _rope_table(inv, pos0, seq, tb):
    half = inv.shape[0]
    spec = pl.BlockSpec((tb, half), lambda i: (i, 0))
    return pl.pallas_call(
        functools.partial(_rope_table_kernel, pos0=pos0),
        grid=(seq // tb,),
        in_specs=[_resident((1, half))],
        out_specs=[spec, spec],
        out_shape=[jax.ShapeDtypeStruct((seq, half), F32)] * 2,
        compiler_params=_cparams("parallel"),
        name="rope_table",
    )(inv.reshape(1, half))


def _ret_kernel(p_ref, cos_ref, sin_ref, gn_ref, s0_ref, o_ref, sout_ref, st_ref, *, L, nchunk):
    t = pl.program_id(1)
    nq = RET_HEADS * RET_DK
    nv = RET_HEADS * RET_DV
    half = RET_DK // 2

    @pl.when(t == 0)
    def _():
        st_ref[...] = s0_ref[0]

    ti = lax.broadcasted_iota(jnp.int32, (L, 1), 0).astype(F32)
    diff = (lax.broadcasted_iota(jnp.int32, (L, L), 0)
            - lax.broadcasted_iota(jnp.int32, (L, L), 1)).astype(F32)
    scale = RET_DK ** -0.5

    def chunk(c, carry):
        rows = pl.ds(pl.multiple_of(c * L, L), L)
        cos = cos_ref[rows, :]
        sin = sin_ref[rows, :]
        for h in range(RET_HEADS):
            lg = math.log(1.0 - 2.0 ** (-5.0 - h))
            q1 = p_ref[rows, h * RET_DK:h * RET_DK + half]
            q2 = p_ref[rows, h * RET_DK + half:(h + 1) * RET_DK]
            k1 = p_ref[rows, nq + h * RET_DK:nq + h * RET_DK + half]
            k2 = p_ref[rows, nq + h * RET_DK + half:nq + (h + 1) * RET_DK]
            qr = jnp.concatenate([q1 * cos - q2 * sin, q1 * sin + q2 * cos], axis=1).astype(BF16)
            kr = jnp.concatenate([k1 * cos - k2 * sin, k1 * sin + k2 * cos], axis=1) * scale
            v = p_ref[rows, 2 * nq + h * RET_DV:2 * nq + (h + 1) * RET_DV].astype(BF16)
            decay = jnp.where(diff >= 0, jnp.exp(lg * jnp.maximum(diff, 0.0)), 0.0)
            sc = (_dot_nt(qr, kr.astype(BF16)) * decay).astype(BF16)
            st = st_ref[h]
            o = _dot(sc, v) + _dot(qr, st.astype(BF16)) * jnp.exp(lg * (ti + 1.0))
            kd = (kr * jnp.exp(lg * (L - 1.0 - ti))).astype(BF16)
            st_ref[h] = math.exp(lg * L) * st + _dot_tn(kd, v)
            mu = jnp.mean(o, axis=-1, keepdims=True)
            d = o - mu
            var = jnp.mean(d * d, axis=-1, keepdims=True)
            on = d * lax.rsqrt(var + EPS) * gn_ref[:, h * RET_DV:(h + 1) * RET_DV]
            gate = p_ref[rows, 2 * nq + nv + h * RET_DV:2 * nq + nv + (h + 1) * RET_DV]
            o_ref[rows, h * RET_DV:(h + 1) * RET_DV] = (on * (gate * _sigmoid(gate))).astype(BF16)
        return carry

    lax.fori_loop(0, nchunk, chunk, 0)

    @pl.when(t == pl.num_programs(1) - 1)
    def _():
        sout_ref[0] = st_ref[...]


def _ret_core(proj, cos, sin, gn_gain, state, bsz, seq, tm, chunk):
    nt = seq // tm
    w = proj.shape[1]
    nv = RET_HEADS * RET_DV
    half = RET_DK // 2
    st_shape = (1, RET_HEADS, RET_DK, RET_DV)
    return pl.pallas_call(
        functools.partial(_ret_kernel, L=chunk, nchunk=tm // chunk),
        grid=(bsz, nt),
        in_specs=[pl.BlockSpec((tm, w), lambda b, t: (b * nt + t, 0)),
                  pl.BlockSpec((tm, half), lambda b, t: (t, 0)),
                  pl.BlockSpec((tm, half), lambda b, t: (t, 0)),
                  _resident((1, nv)),
                  pl.BlockSpec(st_shape, lambda b, t: (b, 0, 0, 0))],
        out_specs=[pl.BlockSpec((tm, nv), lambda b, t: (b * nt + t, 0)),
                   pl.BlockSpec(st_shape, lambda b, t: (b, 0, 0, 0))],
        out_shape=[jax.ShapeDtypeStruct((bsz * seq, nv), BF16),
                   jax.ShapeDtypeStruct((bsz,) + st_shape[1:], F32)],
        scratch_shapes=[pltpu.VMEM((RET_HEADS, RET_DK, RET_DV), F32)],
        compiler_params=_cparams("parallel", "arbitrary"),
        name="ret_core",
    )(proj, cos, sin, gn_gain.reshape(1, nv), state)


_CONV_PAD = 8


def _conv_kernel(x_ref, g_ref, w_ref, cw_ref, s0_ref, a_ref, sout_ref, z_ref):
    t = pl.program_id(1)
    tm, d = x_ref.shape
    nc = CONV_WIDTH - 1

    @pl.when(t == 0)
    def _():
        z_ref[_CONV_PAD - nc:_CONV_PAD, :] = s0_ref[0]

    h = _rms(x_ref[...], g_ref[...]).astype(BF16)
    b = _dot(h, w_ref[:, 0:d])
    z_ref[_CONV_PAD:_CONV_PAD + tm, :] = _dot(h, w_ref[:, d:2 * d]) * _dot(h, w_ref[:, 2 * d:3 * d])
    y = cw_ref[0:1, :] * z_ref[_CONV_PAD - nc:_CONV_PAD - nc + tm, :]
    for j in range(1, CONV_WIDTH):
        y = y + cw_ref[j:j + 1, :] * z_ref[_CONV_PAD - nc + j:_CONV_PAD - nc + j + tm, :]
    a_ref[...] = (b * y).astype(BF16)
    last = z_ref[_CONV_PAD + tm - nc:_CONV_PAD + tm, :]
    z_ref[_CONV_PAD - nc:_CONV_PAD, :] = last

    @pl.when(t == pl.num_programs(1) - 1)
    def _():
        sout_ref[0] = last


def _conv_core(x, g, w_in, conv_w, state, bsz, seq, tm):
    nt = seq // tm
    d = x.shape[1]
    nc = CONV_WIDTH - 1
    return pl.pallas_call(
        _conv_kernel,
        grid=(bsz, nt),
        in_specs=[pl.BlockSpec((tm, d), lambda b, t: (b * nt + t, 0)),
                  _resident((1, d)), _resident((d, 3 * d)), _resident((CONV_WIDTH, d)),
                  pl.BlockSpec((1, nc, d), lambda b, t: (b, 0, 0))],
        out_specs=[pl.BlockSpec((tm, d), lambda b, t: (b * nt + t, 0)),
                   pl.BlockSpec((1, nc, d), lambda b, t: (b, 0, 0))],
        out_shape=[jax.ShapeDtypeStruct((bsz * seq, d), BF16),
                   jax.ShapeDtypeStruct((bsz, nc, d), F32)],
        scratch_shapes=[pltpu.VMEM((_CONV_PAD + tm, d), F32)],
        compiler_params=_cparams("parallel", "arbitrary"),
        name="conv_core",
    )(x, g.reshape(1, d), w_in, conv_w, state)


_AUG = 6
LOG2E = 1.4426950408889634


def _fox_prep_kernel(x_ref, g_ref, w_ref, wf_ref, bf_ref, qg_ref, kg_ref, sel_ref, selt_ref,
                     qs_ref, kn_ref, kb_ref, v_ref, vb_ref, lf_ref):
    d = x_ref.shape[1]
    h = _rms(x_ref[...], g_ref[...]).astype(BF16)

    def head_norm(a, gain):
        ms = _dot((a * a).astype(BF16), sel_ref[...]) * (1.0 / FOX_DH)
        rs = lax.rsqrt(ms + EPS)
        hi = rs.astype(BF16)
        lo = (rs - hi.astype(F32)).astype(BF16)
        return a * (_dot(hi, selt_ref[...]) + _dot(lo, selt_ref[...])) * gain

    qn = head_norm(_dot(h, w_ref[:, 0:d]), qg_ref[...])
    qs_ref[...] = (qn * (FOX_DH ** -0.5 * LOG2E)).astype(BF16)
    kn = head_norm(_dot(h, w_ref[:, d:2 * d]), kg_ref[...])
    kn_ref[...] = kn
    kb_ref[...] = kn.astype(BF16)
    v = _dot(h, w_ref[:, 2 * d:3 * d])
    v_ref[...] = v
    vb_ref[...] = v.astype(BF16)
    fl = _dot(h, wf_ref[...]) + bf_ref[...]
    ls = jnp.minimum(fl, 0.0) - jnp.log(1.0 + jnp.exp(-jnp.abs(fl)))
    lane = lax.broadcasted_iota(jnp.int32, fl.shape, 1)
    lf_ref[...] = jnp.where(lane < FOX_HEADS, ls, 0.0)


def _fox_prep(x, g, w_qkv, w_f, b_f, q_gain, k_gain, tm):
    m, d = x.shape
    col = np.arange(d) // FOX_DH
    sel = (col[:, None] == np.arange(LANES)[None, :]).astype(np.float32)
    row = pl.BlockSpec((tm, d), lambda i: (i, 0))
    return pl.pallas_call(
        _fox_prep_kernel,
        grid=(m // tm,),
        in_specs=[row, _resident((1, d)), _resident((d, 3 * d)), _resident((d, LANES)),
                  _resident((1, LANES)), _resident((1, d)), _resident((1, d)),
                  _resident((d, LANES)), _resident((LANES, d))],
        out_specs=[row, row, row, row, row, pl.BlockSpec((tm, LANES), lambda i: (i, 0))],
        out_shape=[jax.ShapeDtypeStruct((m, d), BF16), jax.ShapeDtypeStruct((m, d), F32),
                   jax.ShapeDtypeStruct((m, d), BF16), jax.ShapeDtypeStruct((m, d), F32),
                   jax.ShapeDtypeStruct((m, d), BF16), jax.ShapeDtypeStruct((m, LANES), F32)],
        compiler_params=_cparams("parallel"),
        name="fox_prep",
    )(x, g.reshape(1, d), w_qkv, w_f, b_f, jnp.tile(q_gain, FOX_HEADS).reshape(1, d),
      jnp.tile(k_gain, FOX_HEADS).reshape(1, d), jnp.asarray(sel, BF16), jnp.asarray(sel.T, BF16))


def _fox_gate_kernel(lf_ref, shift_ref, pq_ref, pk_ref, cq_ref, ck_ref, qa_ref, ka_ref, carry_ref):
    t = pl.program_id(1)
    cb = lf_ref.shape[1]

    @pl.when(t == 0)
    def _():
        carry_ref[...] = jnp.zeros_like(carry_ref)

    F = _cumsum_rows(_tri(cb).astype(BF16), lf_ref[0]) + carry_ref[...]
    carry_ref[...] = F[cb - 1:cb]
    F2 = F * LOG2E
    qa = cq_ref[...]
    ka = ck_ref[...]
    for j, (pq, pk) in enumerate(zip(_split3(F2 - shift_ref[...]), _split3(F2))):
        qa = qa + _dot(pq, pq_ref[j])
        ka = ka + _dot(pk, pk_ref[j])
    qa_ref[0] = qa.astype(BF16)
    ka_ref[0] = ka.astype(BF16)


def _fox_gate(lf_all, shift, cb):
    bsz, tk, _ = lf_all.shape
    place_q = np.zeros((3, LANES, LANES), np.float32)
    place_k = np.zeros((3, LANES, LANES), np.float32)
    const_q = np.zeros((1, LANES), np.float32)
    const_k = np.zeros((1, LANES), np.float32)
    for h in range(FOX_HEADS):
        for j in range(3):
            place_q[j, h, h * _AUG + j] = 1.0
            place_k[j, h, h * _AUG + 3 + j] = -1.0
            const_q[0, h * _AUG + 3 + j] = 1.0
            const_k[0, h * _AUG + j] = 1.0
    spec = pl.BlockSpec((1, cb, LANES), lambda b, t: (b, t, 0))
    return pl.pallas_call(
        _fox_gate_kernel,
        grid=(bsz, tk // cb),
        in_specs=[spec, _resident((1, LANES)), _resident(place_q.shape), _resident(place_k.shape),
                  _resident(const_q.shape), _resident(const_k.shape)],
        out_specs=[spec, spec],
        out_shape=[jax.ShapeDtypeStruct((bsz, tk, LANES), BF16)] * 2,
        scratch_shapes=[pltpu.VMEM((1, LANES), F32)],
        compiler_params=_cparams("parallel", "arbitrary"),
        name="fox_gate",
    )(lf_all, jnp.broadcast_to(shift.astype(F32), (1, LANES)),
      jnp.asarray(place_q, BF16), jnp.asarray(place_k, BF16),
      jnp.asarray(const_q), jnp.asarray(const_k))


_FOX_MAX_BOUND = 40.0


def _fox_score_bound(q_gain, k_gain, cache_lf):
    gq = jnp.max(jnp.abs(q_gain))
    gk = jnp.max(jnp.abs(k_gain))
    bound = (math.sqrt(FOX_DH) * 1.02 * LOG2E) * gq * gk
    ok = bound <= _FOX_MAX_BOUND
    if cache_lf.size:
        ok = ok & (jnp.max(cache_lf) <= 0.0)
    return jnp.where(ok, bound, 0.0), ok


def _fox_query_operands(q_ref, qa_ref, blk):
    pair = pl.program_id(1)
    lane = lax.broadcasted_iota(jnp.int32, (blk, LANES), 1)
    low = lane < FOX_DH
    q = q_ref[0]
    qa = qa_ref[0]
    qops = []
    for s in range(2):
        lo = (2 * pair + s) * _AUG
        qh = jnp.where(low if s == 0 else ~low, q, jnp.zeros_like(q))
        qah = jnp.where((lane >= lo) & (lane < lo + _AUG), qa, jnp.zeros_like(qa))
        qops.append(jnp.concatenate([qh, qah], axis=1))
    return qops, low


def _fox_key_blocks(step, carry, blk, pblk, past):
    qi = pl.program_id(2)
    if past:
        carry = lax.fori_loop(
            0, past // pblk,
            lambda j, c: step(pl.multiple_of(j * pblk, pblk), pblk, c, False), carry)
    wide = 2 * blk
    carry = lax.fori_loop(
        0, qi // 2, lambda j, c: step(pl.multiple_of(past + j * wide, wide), wide, c, False), carry)
    carry = lax.fori_loop(
        0, qi % 2, lambda j, c: step(pl.multiple_of(past + (qi - 1) * blk, blk), blk, c, False),
        carry)
    return step(pl.multiple_of(past + qi * blk, blk), blk, carry, True)


def _fox_attn_online_kernel(q_ref, qa_ref, k_ref, ka_ref, v_ref, o_ref, *, blk, pblk, past):
    qops, low = _fox_query_operands(q_ref, qa_ref, blk)
    causal = _tri(blk)

    def step(start, size, carry, masked):
        rows = pl.ds(start, size)
        kop = jnp.concatenate([k_ref[0, rows, :], ka_ref[0, rows, :]], axis=1)
        v = v_ref[0, rows, :]
        out = []
        for s in range(2):
            m, l, acc = carry[s]
            sc = _dot_nt(qops[s], kop)
            if masked:
                sc = jnp.where(causal, sc, -jnp.inf)
            m_new = jnp.maximum(m, jnp.max(sc, axis=-1, keepdims=True))
            alpha = jnp.exp2(m - m_new)
            p = jnp.exp2(sc - m_new)
            l = alpha * l + jnp.sum(p, axis=-1, keepdims=True)
            acc = alpha * acc + _dot(p.astype(BF16), v)
            out.append((m_new, l, acc))
        return tuple(out)

    init = tuple((jnp.full((blk, 1), -jnp.inf, F32), jnp.zeros((blk, 1), F32),
                  jnp.zeros((blk, LANES), F32)) for _ in range(2))
    (_, l0, a0), (_, l1, a1) = _fox_key_blocks(step, init, blk, pblk, past)
    o_ref[0] = jnp.where(low, a0 / l0, a1 / l1).astype(BF16)


def _fox_attn_bounded_kernel(q_ref, qa_ref, k_ref, ka_ref, v_ref, o_ref, *, blk, pblk, past):
    qops, low = _fox_query_operands(q_ref, qa_ref, blk)
    qop = jnp.concatenate(qops, axis=0)
    causal = jnp.concatenate([_tri(blk)] * 2, axis=0)

    def step(start, size, carry, masked):
        rows = pl.ds(start, size)
        kop = jnp.concatenate([k_ref[0, rows, :], ka_ref[0, rows, :]], axis=1)
        v = v_ref[0, rows, :]
        lowk = lax.broadcasted_iota(jnp.int32, v.shape, 1) < FOX_DH
        ones = jnp.ones_like(v)
        sc = _dot_nt(qop, kop)
        if masked:
            sc = jnp.where(causal, sc, -jnp.inf)
        p = jnp.exp2(sc).astype(BF16)
        a0, a1 = carry
        return (a0 + _dot(p[:blk], jnp.where(lowk, v, ones)),
                a1 + _dot(p[blk:], jnp.where(lowk, ones, v)))

    zero = jnp.zeros((blk, LANES), F32)
    a0, a1 = _fox_key_blocks(step, (zero, zero), blk, pblk, past)
    o_ref[0] = jnp.where(low, a0 / pltpu.roll(a0, FOX_DH, axis=1),
                         a1 / pltpu.roll(a1, FOX_DH, axis=1)).astype(BF16)


def _fox_attn(body, qs, qa, kb, ka, vb, bsz, seq, past, blk, pblk):
    tk = past + seq
    d = qs.shape[-1]
    pb = past // blk
    qspec = pl.BlockSpec((1, blk, LANES), lambda b, p, i: (b, i, p))
    kspec = pl.BlockSpec((1, tk, LANES), lambda b, p, i: (b, 0, p))
    return pl.pallas_call(
        functools.partial(body, blk=blk, pblk=pblk, past=past),
        grid=(bsz, d // LANES, seq // blk),
        in_specs=[qspec,
                  pl.BlockSpec((1, blk, LANES), lambda b, p, i: (b, pb + i, 0)),
                  kspec,
                  pl.BlockSpec((1, tk, LANES), lambda b, p, i: (b, 0, 0)),
                  kspec],
        out_specs=qspec,
        out_shape=jax.ShapeDtypeStruct((bsz, seq, d), BF16),
        compiler_params=_cparams("parallel", "parallel", "arbitrary"),
        name=body.__name__.strip("_").replace("_kernel", ""),
    )(qs, qa, kb, ka, vb)


def _trunk(x, pos0, hg_s, ret_s, conv_s, fox_k, fox_v, fox_lf, W):
    bsz, seq, d = x.shape
    m = bsz * seq
    past = fox_k.shape[1]
    tm = min(seq, 256)
    tm_ffn = min(m, 512)
    x = x.reshape(m, d)

    def ffn(x, a, w_out, layer):
        return _outproj_ffn(x, a, w_out, W["norm_ffn"][layer], W["ffn_w_g"][layer],
                            W["ffn_w_u"][layer], W["ffn_w_d"][layer], tm_ffn, D_FF // 2)

    proj = _norm_proj(x, W["norm_mix"][0], W["hg_w_in"], tm, 512)
    a, hg_s = _hgrn_core(proj, W["hg_lower_bounds"], W["hg_out_gain"], hg_s, 0, bsz, seq, tm)
    x = ffn(x, a, W["hg_w_out"], 0)

    proj = _norm_proj(x, W["norm_mix"][1], W["ret_w_in"], tm, 512)
    inv = jnp.power(ROPE_BASE, -jnp.arange(0, RET_DK, 2, dtype=F32) / RET_DK)
    cos, sin = _rope_table(inv, pos0, seq, tm)
    a, ret_s = _ret_core(proj, cos, sin, W["ret_gn_gain"], ret_s, bsz, seq, tm, min(seq, 128))
    x = ffn(x, a, W["ret_w_out"], 1)

    a, conv_s = _conv_core(x, W["norm_mix"][2], W["conv_w_in"], W["conv_w"], conv_s, bsz, seq, tm)
    x = ffn(x, a, W["conv_w_out"], 2)

    qs, kn, kb, v, vb, lf = _fox_prep(x, W["norm_mix"][3], W["fox_w_qkv"], W["fox_w_f"],
                                      W["fox_b_f"], W["fox_q_gain"], W["fox_k_gain"], tm)
    blk = min(seq, 512)
    lf3 = lf.reshape(bsz, seq, LANES)
    kb3 = kb.reshape(bsz, seq, d)
    vb3 = vb.reshape(bsz, seq, d)
    if past:
        lf3 = jnp.concatenate([jnp.pad(fox_lf, ((0, 0), (0, 0), (0, LANES - FOX_HEADS))), lf3], axis=1)
        kb3 = jnp.concatenate([fox_k.reshape(bsz, past, d).astype(BF16), kb3], axis=1)
        vb3 = jnp.concatenate([fox_v.reshape(bsz, past, d).astype(BF16), vb3], axis=1)
    shift, bounded = _fox_score_bound(W["fox_q_gain"], W["fox_k_gain"], fox_lf)
    qa, ka = _fox_gate(lf3, shift, min(seq, 256))
    attn = functools.partial(_fox_attn, bsz=bsz, seq=seq, past=past, blk=blk,
                             pblk=min(past, 512) if past else blk)
    a = lax.cond(bounded, functools.partial(attn, _fox_attn_bounded_kernel),
                 functools.partial(attn, _fox_attn_online_kernel),
                 qs.reshape(bsz, seq, d), qa, kb3, ka, vb3)
    x = ffn(x, a.reshape(m, d), W["fox_w_out"], 3)

    return (x.reshape(bsz, seq, d), hg_s, ret_s, conv_s,
            kn.reshape(bsz, seq, FOX_HEADS, FOX_DH), v.reshape(bsz, seq, FOX_HEADS, FOX_DH),
            lf[:, :FOX_HEADS].reshape(bsz, seq, FOX_HEADS))


def kernel(x_prompt, x_sample, state_hgrn, state_ret, state_conv, cache_fox_k, cache_fox_v, cache_fox_logf, norm_mix, norm_ffn, hg_w_in, hg_lower_bounds, hg_out_gain, hg_w_out, ret_w_in, ret_gn_gain, ret_w_out, conv_w_in, conv_w, conv_w_out, fox_w_in, fox_b_f, fox_q_gain, fox_k_gain, fox_w_out, ffn_w_gu, ffn_w_down):
    d = D_MODEL
    bf = lambda w: w.astype(BF16)
    W = dict(
        norm_mix=norm_mix, norm_ffn=norm_ffn,
        hg_w_in=bf(hg_w_in), hg_lower_bounds=hg_lower_bounds, hg_out_gain=hg_out_gain,
        hg_w_out=bf(hg_w_out),
        ret_w_in=bf(ret_w_in), ret_gn_gain=ret_gn_gain, ret_w_out=bf(ret_w_out),
        conv_w_in=bf(conv_w_in), conv_w=conv_w, conv_w_out=bf(conv_w_out),
        fox_w_qkv=bf(fox_w_in[:, :3 * d]),
        fox_w_f=bf(jnp.pad(fox_w_in[:, 3 * d:], ((0, 0), (0, LANES - FOX_HEADS)))),
        fox_b_f=jnp.pad(fox_b_f, (0, LANES - FOX_HEADS)).reshape(1, LANES),
        fox_q_gain=fox_q_gain, fox_k_gain=fox_k_gain, fox_w_out=bf(fox_w_out),
        ffn_w_g=bf(ffn_w_gu[:, :, :D_FF]), ffn_w_u=bf(ffn_w_gu[:, :, D_FF:]), ffn_w_d=bf(ffn_w_down),
    )
    bsz = x_prompt.shape[0]
    dt = x_prompt.dtype
    (y_p, hg_p, ret_p, conv_p, fk_p, fv_p, flf_p) = _trunk(
        x_prompt, 0,
        jnp.zeros((bsz, HG_HEADS, HG_DK, HG_DV), F32),
        jnp.zeros((bsz, RET_HEADS, RET_DK, RET_DV), F32),
        jnp.zeros((bsz, CONV_WIDTH - 1, d), dt),
        jnp.zeros((bsz, 0, FOX_HEADS, FOX_DH), dt),
        jnp.zeros((bsz, 0, FOX_HEADS, FOX_DH), dt),
        jnp.zeros((bsz, 0, FOX_HEADS), F32), W)
    past = cache_fox_k.shape[1]
    (y_s, hg_s, ret_s, conv_s, fk_s, fv_s, flf_s) = _trunk(
        x_sample, past, state_hgrn, state_ret, state_conv,
        cache_fox_k, cache_fox_v, cache_fox_logf, W)
    return (y_p, y_s, hg_p, hg_s, ret_p, ret_s, conv_p, conv_s,
            fk_p, fv_p, flf_p, fk_s, fv_s, flf_s)
```

```python
import functools
import math

import numpy as np
import jax
import jax.numpy as jnp
from jax import lax
from jax.experimental import pallas as pl
from jax.experimental.pallas import tpu as pltpu

F32 = jnp.float32
BF16 = jnp.bfloat16

D_MODEL = 1024
EPS = 1e-6
HG_HEADS, HG_DK, HG_DV = 8, 128, 128
HG_CHUNK = 64
RET_HEADS, RET_DK, RET_DV = 4, 256, 512
ROPE_BASE = 10000.0
CONV_WIDTH = 3
FOX_HEADS, FOX_DH = 16, 64
D_FF = 2816
MXU_DIM = 256
FFN_BOUNDS = (0, (D_FF // MXU_DIM + 1) // 2 * MXU_DIM, D_FF)
LANES = 128

VMEM_LIMIT = 56 * 1024 * 1024


def _cparams(*sem):
    return pltpu.CompilerParams(dimension_semantics=sem, vmem_limit_bytes=VMEM_LIMIT)


def _resident(shape):
    nd = len(shape)
    return pl.BlockSpec(shape, lambda *_: (0,) * nd, pipeline_mode=pl.Buffered(1))


def _dot(a, b):
    return jnp.dot(a, b, preferred_element_type=F32)


def _dot_nt(a, b):
    return lax.dot_general(a, b, (((1,), (1,)), ((), ())), preferred_element_type=F32)


def _dot_tn(a, b):
    return lax.dot_general(a, b, (((0,), (0,)), ((), ())), preferred_element_type=F32)


def _sigmoid(x):
    return 1.0 / (1.0 + jnp.exp(-x))


def _rms(x, g):
    return x * lax.rsqrt(jnp.mean(x * x, axis=-1, keepdims=True) + EPS) * g


def _split3(x):
    hi = x.astype(BF16)
    r = x - hi.astype(F32)
    mid = r.astype(BF16)
    lo = (r - mid.astype(F32)).astype(BF16)
    return hi, mid, lo


def _tri(n):
    r = lax.broadcasted_iota(jnp.int32, (n, n), 0)
    c = lax.broadcasted_iota(jnp.int32, (n, n), 1)
    return r >= c


def _cumsum_rows(tri_bf16, x):
    hi, mid, lo = _split3(x)
    return _dot(tri_bf16, hi) + _dot(tri_bf16, mid) + _dot(tri_bf16, lo)


def _norm_proj_kernel(x_ref, g_ref, w_ref, o_ref, *, tn):
    h = _rms(x_ref[...], g_ref[...]).astype(BF16)
    for c in range(o_ref.shape[1] // tn):
        o_ref[:, c * tn:(c + 1) * tn] = _dot(h, w_ref[:, c * tn:(c + 1) * tn])


def _norm_proj(x, g, w, tm, tn):
    m, d = x.shape
    n = w.shape[1]
    return pl.pallas_call(
        functools.partial(_norm_proj_kernel, tn=tn),
        grid=(m // tm,),
        in_specs=[pl.BlockSpec((tm, d), lambda i: (i, 0)), _resident((1, d)), _resident((d, n))],
        out_specs=pl.BlockSpec((tm, n), lambda i: (i, 0)),
        out_shape=jax.ShapeDtypeStruct((m, n), F32),
        compiler_params=_cparams("parallel"),
        name="norm_proj",
    )(x, g.reshape(1, d), w)


def _ffn_kernel(x_ref, a_ref, wo_ref, g_ref, wg_ref, wu_ref, wd_ref, y_ref, *, bounds):
    x1 = x_ref[...] + _dot(a_ref[...], wo_ref[...])
    h = _rms(x1, g_ref[...]).astype(BF16)
    acc = x1
    for lo, hi in zip(bounds[:-1], bounds[1:]):
        gate = _dot(h, wg_ref[:, lo:hi])
        up = _dot(h, wu_ref[:, lo:hi])
        act = (gate * _sigmoid(gate) * up).astype(BF16)
        acc = acc + _dot(act, wd_ref[lo:hi, :])
    y_ref[...] = acc


def _outproj_ffn(x, a, w_out, g, w_g, w_u, w_d, tm, bounds):
    m, d = x.shape
    ka = a.shape[1]
    return pl.pallas_call(
        functools.partial(_ffn_kernel, bounds=bounds),
        grid=(m // tm,),
        in_specs=[pl.BlockSpec((tm, d), lambda i: (i, 0)),
                  pl.BlockSpec((tm, ka), lambda i: (i, 0)),
                  _resident((ka, d)), _resident((1, d)),
                  _resident((d, D_FF)), _resident((d, D_FF)), _resident((D_FF, d))],
        out_specs=pl.BlockSpec((tm, d), lambda i: (i, 0)),
        out_shape=jax.ShapeDtypeStruct((m, d), F32),
        compiler_params=_cparams("parallel"),
        name="outproj_ffn",
    )(x, a, w_out, g.reshape(1, d), w_g, w_u, w_d)


def _hgrn_kernel(p_ref, lbw_ref, gain_ref, s0_ref, o_ref, sout_ref, st_ref, *, layer, nchunk):
    t = pl.program_id(1)
    L = HG_CHUNK
    nk = HG_HEADS * HG_DK

    @pl.when(t == 0)
    def _():
        for h in range(HG_HEADS):
            st_ref[h] = s0_ref[0, h].T

    lbw = lbw_ref[...]
    e = jnp.exp(lbw - jnp.max(lbw, axis=0, keepdims=True))
    lb = jnp.sum(e[:layer + 1], axis=0, keepdims=True) / jnp.sum(e, axis=0, keepdims=True)
    gain = gain_ref[...]
    tm = p_ref.shape[0]

    r = lax.broadcasted_iota(jnp.int32, (tm, tm), 0)
    c = lax.broadcasted_iota(jnp.int32, (tm, tm), 1)
    shift = L.bit_length() - 1
    tri = (r >= c) & ((r >> shift) == (c >> shift))
    q = p_ref[:, 0:nk]
    fg = lb + (1.0 - lb) * _sigmoid(p_ref[:, nk:2 * nk])
    k = 1.0 - fg
    G = _cumsum_rows(tri.astype(BF16), jnp.log(fg))

    def per_chunk_row(offset):
        return jnp.concatenate(
            [jnp.broadcast_to(G[i * L + offset:i * L + offset + 1], (L, nk)) for i in range(nchunk)],
            axis=0)

    g_mid = per_chunk_row(L // 2)
    g_last = per_chunk_row(L - 1)
    qa = (q * jnp.exp(G - g_mid)).astype(BF16)
    kb = (k * jnp.exp(g_mid - G)).astype(BF16)
    qs = (q * jnp.exp(G)).astype(BF16)
    kl = (k * jnp.exp(g_last - G)).astype(BF16)
    for h in range(HG_HEADS):
        sl = slice(h * HG_DK, (h + 1) * HG_DK)
        v = p_ref[:, 2 * nk + h * HG_DV:2 * nk + (h + 1) * HG_DV].astype(BF16)
        sc = jnp.where(tri, _dot_nt(qa[:, sl], kb[:, sl]), 0.0).astype(BF16)
        o_intra = _dot(sc, v)
        st = st_ref[h]
        outs = []
        for i in range(nchunk):
            rows = slice(i * L, (i + 1) * L)
            outs.append(o_intra[rows] + _dot_nt(qs[rows, sl], st.astype(BF16)))
            dec = jnp.exp(G[(i + 1) * L - 1:(i + 1) * L, sl])
            st = st * dec + _dot_tn(v[rows], kl[rows, sl])
        st_ref[h] = st
        gate = p_ref[:, 3 * nk + h * HG_DV:3 * nk + (h + 1) * HG_DV]
        o_ref[:, h * HG_DV:(h + 1) * HG_DV] = (
            _rms(jnp.concatenate(outs, axis=0), gain) * (gate * _sigmoid(gate))).astype(BF16)

    @pl.when(t == pl.num_programs(1) - 1)
    def _():
        for h in range(HG_HEADS):
            sout_ref[0, h] = st_ref[h].T


def _hgrn_core(proj, lower_bounds, out_gain, state, layer, bsz, seq, tm):
    nt = seq // tm
    w = proj.shape[1]
    st_shape = (1, HG_HEADS, HG_DK, HG_DV)
    return pl.pallas_call(
        functools.partial(_hgrn_kernel, layer=layer, nchunk=tm // HG_CHUNK),
        grid=(bsz, nt),
        in_specs=[pl.BlockSpec((tm, w), lambda b, t: (b * nt + t, 0)),
                  _resident(lower_bounds.shape), _resident((1, HG_DV)),
                  pl.BlockSpec(st_shape, lambda b, t: (b, 0, 0, 0))],
        out_specs=[pl.BlockSpec((tm, HG_HEADS * HG_DV), lambda b, t: (b * nt + t, 0)),
                   pl.BlockSpec(st_shape, lambda b, t: (b, 0, 0, 0))],
        out_shape=[jax.ShapeDtypeStruct((bsz * seq, HG_HEADS * HG_DV), BF16),
                   jax.ShapeDtypeStruct((bsz,) + st_shape[1:], F32)],
        scratch_shapes=[pltpu.VMEM((HG_HEADS, HG_DV, HG_DK), F32)],
        compiler_params=_cparams("parallel", "arbitrary"),
        name="hgrn_core",
    )(proj, lower_bounds, out_gain.reshape(1, HG_DV), state)


def _rope_table_kernel(inv_ref, cos_ref, sin_ref, *, pos0):
    tb = cos_ref.shape[0]
    pos = (pos0 + pl.program_id(0) * tb
           + lax.broadcasted_iota(jnp.int32, cos_ref.shape, 0)).astype(F32)
    ang = pos * inv_ref[...]
    cos_ref[...] = jnp.cos(ang)
    sin_ref[...] = jnp.sin(ang)


def _rope_table(inv, pos0, seq, tb):
    half = inv.shape[0]
    spec = pl.BlockSpec((tb, half), lambda i: (i, 0))
    return pl.pallas_call(
        functools.partial(_rope_table_kernel, pos0=pos0),
        grid=(seq // tb,),
        in_specs=[_resident((1, half))],
        out_specs=[spec, spec],
        out_shape=[jax.ShapeDtypeStruct((seq, half), F32)] * 2,
        compiler_params=_cparams("parallel"),
        name="rope_table",
    )(inv.reshape(1, half))


def _ret_kernel(p_ref, cos_ref, sin_ref, gn_ref, s0_ref, o_ref, sout_ref, st_ref, *, L, nchunk):
    t = pl.program_id(1)
    nq = RET_HEADS * RET_DK
    nv = RET_HEADS * RET_DV
    half = RET_DK // 2

    @pl.when(t == 0)
    def _():
        st_ref[...] = s0_ref[0]

    ti = lax.broadcasted_iota(jnp.int32, (L, 1), 0).astype(F32)
    diff = (lax.broadcasted_iota(jnp.int32, (L, L), 0)
            - lax.broadcasted_iota(jnp.int32, (L, L), 1)).astype(F32)
    scale = RET_DK ** -0.5

    def chunk(c, carry):
        rows = pl.ds(pl.multiple_of(c * L, L), L)
        cos = cos_ref[rows, :]
        sin = sin_ref[rows, :]
        for h in range(RET_HEADS):
            lg = math.log(1.0 - 2.0 ** (-5.0 - h))
            q1 = p_ref[rows, h * RET_DK:h * RET_DK + half]
            q2 = p_ref[rows, h * RET_DK + half:(h + 1) * RET_DK]
            k1 = p_ref[rows, nq + h * RET_DK:nq + h * RET_DK + half]
            k2 = p_ref[rows, nq + h * RET_DK + half:nq + (h + 1) * RET_DK]
            qr = jnp.concatenate([q1 * cos - q2 * sin, q1 * sin + q2 * cos], axis=1).astype(BF16)
            kr = jnp.concatenate([k1 * cos - k2 * sin, k1 * sin + k2 * cos], axis=1) * scale
            v = p_ref[rows, 2 * nq + h * RET_DV:2 * nq + (h + 1) * RET_DV].astype(BF16)
            decay = jnp.where(diff >= 0, jnp.exp(lg * jnp.maximum(diff, 0.0)), 0.0)
            sc = (_dot_nt(qr, kr.astype(BF16)) * decay).astype(BF16)
            st = st_ref[h]
            o = _dot(sc, v) + _dot(qr, st.astype(BF16)) * jnp.exp(lg * (ti + 1.0))
            kd = (kr * jnp.exp(lg * (L - 1.0 - ti))).astype(BF16)
            st_ref[h] = math.exp(lg * L) * st + _dot_tn(kd, v)
            mu = jnp.mean(o, axis=-1, keepdims=True)
            d = o - mu
            var = jnp.mean(d * d, axis=-1, keepdims=True)
            on = d * lax.rsqrt(var + EPS) * gn_ref[:, h * RET_DV:(h + 1) * RET_DV]
            gate = p_ref[rows, 2 * nq + nv + h * RET_DV:2 * nq + nv + (h + 1) * RET_DV]
            o_ref[rows, h * RET_DV:(h + 1) * RET_DV] = (on * (gate * _sigmoid(gate))).astype(BF16)
        return carry

    lax.fori_loop(0, nchunk, chunk, 0)

    @pl.when(t == pl.num_programs(1) - 1)
    def _():
        sout_ref[0] = st_ref[...]


def _ret_core(proj, cos, sin, gn_gain, state, bsz, seq, tm, chunk):
    nt = seq // tm
    w = proj.shape[1]
    nv = RET_HEADS * RET_DV
    half = RET_DK // 2
    st_shape = (1, RET_HEADS, RET_DK, RET_DV)
    return pl.pallas_call(
        functools.partial(_ret_kernel, L=chunk, nchunk=tm // chunk),
        grid=(bsz, nt),
        in_specs=[pl.BlockSpec((tm, w), lambda b, t: (b * nt + t, 0)),
                  pl.BlockSpec((tm, half), lambda b, t: (t, 0)),
                  pl.BlockSpec((tm, half), lambda b, t: (t, 0)),
                  _resident((1, nv)),
                  pl.BlockSpec(st_shape, lambda b, t: (b, 0, 0, 0))],
        out_specs=[pl.BlockSpec((tm, nv), lambda b, t: (b * nt + t, 0)),
                   pl.BlockSpec(st_shape, lambda b, t: (b, 0, 0, 0))],
        out_shape=[jax.ShapeDtypeStruct((bsz * seq, nv), BF16),
                   jax.ShapeDtypeStruct((bsz,) + st_shape[1:], F32)],
        scratch_shapes=[pltpu.VMEM((RET_HEADS, RET_DK, RET_DV), F32)],
        compiler_params=_cparams("parallel", "arbitrary"),
        name="ret_core",
    )(proj, cos, sin, gn_gain.reshape(1, nv), state)


_CONV_PAD = 8


def _conv_kernel(x_ref, g_ref, w_ref, cw_ref, s0_ref, a_ref, sout_ref, z_ref):
    t = pl.program_id(1)
    tm, d = x_ref.shape
    nc = CONV_WIDTH - 1

    @pl.when(t == 0)
    def _():
        z_ref[_CONV_PAD - nc:_CONV_PAD, :] = s0_ref[0]

    h = _rms(x_ref[...], g_ref[...]).astype(BF16)
    b = _dot(h, w_ref[:, 0:d])
    z_ref[_CONV_PAD:_CONV_PAD + tm, :] = _dot(h, w_ref[:, d:2 * d]) * _dot(h, w_ref[:, 2 * d:3 * d])
    y = cw_ref[0:1, :] * z_ref[_CONV_PAD - nc:_CONV_PAD - nc + tm, :]
    for j in range(1, CONV_WIDTH):
        y = y + cw_ref[j:j + 1, :] * z_ref[_CONV_PAD - nc + j:_CONV_PAD - nc + j + tm, :]
    a_ref[...] = (b * y).astype(BF16)
    last = z_ref[_CONV_PAD + tm - nc:_CONV_PAD + tm, :]
    z_ref[_CONV_PAD - nc:_CONV_PAD, :] = last

    @pl.when(t == pl.num_programs(1) - 1)
    def _():
        sout_ref[0] = last


def _conv_core(x, g, w_in, conv_w, state, bsz, seq, tm):
    nt = seq // tm
    d = x.shape[1]
    nc = CONV_WIDTH - 1
    return pl.pallas_call(
        _conv_kernel,
        grid=(bsz, nt),
        in_specs=[pl.BlockSpec((tm, d), lambda b, t: (b * nt + t, 0)),
                  _resident((1, d)), _resident((d, 3 * d)), _resident((CONV_WIDTH, d)),
                  pl.BlockSpec((1, nc, d), lambda b, t: (b, 0, 0))],
        out_specs=[pl.BlockSpec((tm, d), lambda b, t: (b * nt + t, 0)),
                   pl.BlockSpec((1, nc, d), lambda b, t: (b, 0, 0))],
        out_shape=[jax.ShapeDtypeStruct((bsz * seq, d), BF16),
                   jax.ShapeDtypeStruct((bsz, nc, d), F32)],
        scratch_shapes=[pltpu.VMEM((_CONV_PAD + tm, d), F32)],
        compiler_params=_cparams("parallel", "arbitrary"),
        name="conv_core",
    )(x, g.reshape(1, d), w_in, conv_w, state)


_AUG = 6
LOG2E = 1.4426950408889634


def _fox_prep_kernel(x_ref, g_ref, w_ref, wf_ref, bf_ref, qg_ref, kg_ref, sel_ref, selt_ref,
                     qs_ref, kn_ref, kb_ref, v_ref, vb_ref, vt_ref, lf_ref):
    d = x_ref.shape[1]
    h = _rms(x_ref[...], g_ref[...]).astype(BF16)

    def head_norm(a, gain):
        ms = _dot((a * a).astype(BF16), sel_ref[...]) * (1.0 / FOX_DH)
        rs = lax.rsqrt(ms + EPS)
        hi = rs.astype(BF16)
        lo = (rs - hi.astype(F32)).astype(BF16)
        return a * (_dot(hi, selt_ref[...]) + _dot(lo, selt_ref[...])) * gain

    qn = head_norm(_dot(h, w_ref[:, 0:d]), qg_ref[...])
    qs_ref[...] = (qn * (FOX_DH ** -0.5 * LOG2E)).astype(BF16)
    kn = head_norm(_dot(h, w_ref[:, d:2 * d]), kg_ref[...])
    kn_ref[...] = kn
    kb_ref[...] = kn.astype(BF16)
    v = _dot(h, w_ref[:, 2 * d:3 * d])
    v_ref[...] = v
    vb_ref[...] = v.astype(BF16)
    vt_ref[0] = v.T.astype(BF16)
    fl = _dot(h, wf_ref[...]) + bf_ref[...]
    ls = jnp.minimum(fl, 0.0) - jnp.log(1.0 + jnp.exp(-jnp.abs(fl)))
    lane = lax.broadcasted_iota(jnp.int32, fl.shape, 1)
    lf_ref[...] = jnp.where(lane < FOX_HEADS, ls, 0.0)


def _fox_prep(x, g, w_qkv, w_f, b_f, q_gain, k_gain, bsz, tm):
    m, d = x.shape
    nt = m // bsz // tm
    col = np.arange(d) // FOX_DH
    sel = (col[:, None] == np.arange(LANES)[None, :]).astype(np.float32)
    row = pl.BlockSpec((tm, d), lambda i: (i, 0))
    return pl.pallas_call(
        _fox_prep_kernel,
        grid=(m // tm,),
        in_specs=[row, _resident((1, d)), _resident((d, 3 * d)), _resident((d, LANES)),
                  _resident((1, LANES)), _resident((1, d)), _resident((1, d)),
                  _resident((d, LANES)), _resident((LANES, d))],
        out_specs=[row, row, row, row, row,
                   pl.BlockSpec((1, d, tm), lambda i: (i // nt, 0, i % nt)),
                   pl.BlockSpec((tm, LANES), lambda i: (i, 0))],
        out_shape=[jax.ShapeDtypeStruct((m, d), BF16), jax.ShapeDtypeStruct((m, d), F32),
                   jax.ShapeDtypeStruct((m, d), BF16), jax.ShapeDtypeStruct((m, d), F32),
                   jax.ShapeDtypeStruct((m, d), BF16),
                   jax.ShapeDtypeStruct((bsz, d, m // bsz), BF16),
                   jax.ShapeDtypeStruct((m, LANES), F32)],
        compiler_params=_cparams("parallel"),
        name="fox_prep",
    )(x, g.reshape(1, d), w_qkv, w_f, b_f, jnp.tile(q_gain, FOX_HEADS).reshape(1, d),
      jnp.tile(k_gain, FOX_HEADS).reshape(1, d), jnp.asarray(sel, BF16), jnp.asarray(sel.T, BF16))


def _fox_gate_kernel(lf_ref, shift_ref, pq_ref, pk_ref, cq_ref, ck_ref, qa_ref, ka_ref, carry_ref):
    t = pl.program_id(1)
    cb = lf_ref.shape[1]

    @pl.when(t == 0)
    def _():
        carry_ref[...] = jnp.zeros_like(carry_ref)

    F = _cumsum_rows(_tri(cb).astype(BF16), lf_ref[0]) + carry_ref[...]
    carry_ref[...] = F[cb - 1:cb]
    F2 = F * LOG2E
    qa = cq_ref[...]
    ka = ck_ref[...]
    for j, (pq, pk) in enumerate(zip(_split3(F2 - shift_ref[...]), _split3(F2))):
        qa = qa + _dot(pq, pq_ref[j])
        ka = ka + _dot(pk, pk_ref[j])
    qa_ref[0] = qa.astype(BF16)
    ka_ref[0] = ka.astype(BF16)


def _fox_gate(lf_all, shift, cb):
    bsz, tk, _ = lf_all.shape
    place_q = np.zeros((3, LANES, LANES), np.float32)
    place_k = np.zeros((3, LANES, LANES), np.float32)
    const_q = np.zeros((1, LANES), np.float32)
    const_k = np.zeros((1, LANES), np.float32)
    for h in range(FOX_HEADS):
        for j in range(3):
            place_q[j, h, h * _AUG + j] = 1.0
            place_k[j, h, h * _AUG + 3 + j] = -1.0
            const_q[0, h * _AUG + 3 + j] = 1.0
            const_k[0, h * _AUG + j] = 1.0
    spec = pl.BlockSpec((1, cb, LANES), lambda b, t: (b, t, 0))
    return pl.pallas_call(
        _fox_gate_kernel,
        grid=(bsz, tk // cb),
        in_specs=[spec, _resident((1, LANES)), _resident(place_q.shape), _resident(place_k.shape),
                  _resident(const_q.shape), _resident(const_k.shape)],
        out_specs=[spec, spec],
        out_shape=[jax.ShapeDtypeStruct((bsz, tk, LANES), BF16)] * 2,
        scratch_shapes=[pltpu.VMEM((1, LANES), F32)],
        compiler_params=_cparams("parallel", "arbitrary"),
        name="fox_gate",
    )(lf_all, jnp.broadcast_to(shift.astype(F32), (1, LANES)),
      jnp.asarray(place_q, BF16), jnp.asarray(place_k, BF16),
      jnp.asarray(const_q), jnp.asarray(const_k))


_FOX_MAX_BOUND = 40.0


def _fox_score_bound(q_gain, k_gain, cache_lf):
    gq = jnp.max(jnp.abs(q_gain))
    gk = jnp.max(jnp.abs(k_gain))
    bound = (math.sqrt(FOX_DH) * 1.02 * LOG2E) * gq * gk
    ok = bound <= _FOX_MAX_BOUND
    if cache_lf.size:
        ok = ok & (jnp.max(cache_lf) <= 0.0)
    return jnp.where(ok, bound, 0.0), ok


def _fox_query_operands(q_ref, qa_ref, blk):
    pair = pl.program_id(1)
    lane = lax.broadcasted_iota(jnp.int32, (blk, LANES), 1)
    low = lane < FOX_DH
    q = q_ref[0]
    qa = qa_ref[0]
    qops = []
    for s in range(2):
        lo = (2 * pair + s) * _AUG
        qh = jnp.where(low if s == 0 else ~low, q, jnp.zeros_like(q))
        qah = jnp.where((lane >= lo) & (lane < lo + _AUG), qa, jnp.zeros_like(qa))
        qops.append(jnp.concatenate([qh, qah], axis=1))
    return qops, low


def _fox_key_blocks(step, carry, blk, pblk, past, nq):
    if past:
        carry = lax.fori_loop(
            0, past // pblk,
            lambda j, c: step(pl.multiple_of(j * pblk, pblk), pblk, c, False), carry)
    if nq == 1:
        return step(past, blk, carry, True)
    qi = pl.program_id(2)
    wide = 2 * blk

    def quad(j, c):
        start = pl.multiple_of(past + j * 2 * wide, wide)
        return step(start + wide, wide, step(start, wide, c, False), False)

    carry = lax.fori_loop(0, qi // 4, quad, carry)
    carry = lax.fori_loop(
        0, (qi % 4) // 2,
        lambda j, c: step(pl.multiple_of(past + (qi // 4) * 2 * wide, wide), wide, c, False), carry)
    carry = lax.fori_loop(
        0, qi % 2, lambda j, c: step(pl.multiple_of(past + (qi - 1) * blk, blk), blk, c, False),
        carry)
    return step(pl.multiple_of(past + qi * blk, blk), blk, carry, True)


def _fox_attn_online_kernel(q_ref, qa_ref, k_ref, ka_ref, v_ref, o_ref, *, blk, pblk, past, nq):
    qops, low = _fox_query_operands(q_ref, qa_ref, blk)
    causal = _tri(blk)

    def step(start, size, carry, masked):
        rows = pl.ds(start, size)
        kop = jnp.concatenate([k_ref[0, rows, :], ka_ref[0, rows, :]], axis=1)
        v = v_ref[0, rows, :]
        out = []
        for s in range(2):
            m, l, acc = carry[s]
            sc = _dot_nt(qops[s], kop)
            if masked:
                sc = jnp.where(causal, sc, -jnp.inf)
            m_new = jnp.maximum(m, jnp.max(sc, axis=-1, keepdims=True))
            alpha = jnp.exp2(m - m_new)
            p = jnp.exp2(sc - m_new)
            l = alpha * l + jnp.sum(p, axis=-1, keepdims=True)
            acc = alpha * acc + _dot(p.astype(BF16), v)
            out.append((m_new, l, acc))
        return tuple(out)

    init = tuple((jnp.full((blk, 1), -jnp.inf, F32), jnp.zeros((blk, 1), F32),
                  jnp.zeros((blk, LANES), F32)) for _ in range(2))
    (_, l0, a0), (_, l1, a1) = _fox_key_blocks(step, init, blk, pblk, past, nq)
    o_ref[0] = jnp.where(low, a0 / l0, a1 / l1).astype(BF16)


def _fox_attn_bounded_kernel(q_ref, qa_ref, k_ref, ka_ref, vt_ref, o_ref, *, blk, pblk, past,
                             nq):
    qops, _ = _fox_query_operands(q_ref, qa_ref, blk)
    qop = jnp.concatenate(qops, axis=0)
    r = lax.broadcasted_iota(jnp.int32, (blk, 2 * blk), 0)
    c = lax.broadcasted_iota(jnp.int32, (blk, 2 * blk), 1)
    causal = r <= jnp.where(c >= blk, c - blk, c)
    chan = lax.broadcasted_iota(jnp.int32, (LANES, 1), 0) < FOX_DH

    def step(start, size, carry, masked):
        rows = pl.ds(start, size)
        kop = jnp.concatenate([k_ref[0, rows, :], ka_ref[0, rows, :]], axis=1)
        vt = vt_ref[0, :, rows]
        ones = jnp.ones_like(vt)
        sc = _dot_nt(kop, qop)
        if masked:
            sc = jnp.where(causal, sc, -jnp.inf)
        p = jnp.exp2(sc).astype(BF16)
        a0, a1 = carry
        return (a0 + _dot(jnp.where(chan, vt, ones), p[:, :blk]),
                a1 + _dot(jnp.where(chan, ones, vt), p[:, blk:]))

    zero = jnp.zeros((LANES, blk), F32)
    a0, a1 = _fox_key_blocks(step, (zero, zero), blk, pblk, past, nq)
    out_t = jnp.concatenate([a0[:FOX_DH] / a0[FOX_DH:], a1[FOX_DH:] / a1[:FOX_DH]], axis=0)
    o_ref[0] = out_t.T.astype(BF16)


def _fox_attn(body, transposed_v, qs, qa, kb, ka, v, bsz, seq, past, blk, pblk):
    tk = past + seq
    d = qs.shape[-1]
    pb = past // blk
    qspec = pl.BlockSpec((1, blk, LANES), lambda b, p, i: (b, i, p))
    kspec = pl.BlockSpec((1, tk, LANES), lambda b, p, i: (b, 0, p))
    vspec = pl.BlockSpec((1, LANES, tk), lambda b, p, i: (b, p, 0)) if transposed_v else kspec
    return pl.pallas_call(
        functools.partial(body, blk=blk, pblk=pblk, past=past, nq=seq // blk),
        grid=(bsz, d // LANES, seq // blk),
        in_specs=[qspec,
                  pl.BlockSpec((1, blk, LANES), lambda b, p, i: (b, pb + i, 0)),
                  kspec,
                  pl.BlockSpec((1, tk, LANES), lambda b, p, i: (b, 0, 0)),
                  vspec],
        out_specs=qspec,
        out_shape=jax.ShapeDtypeStruct((bsz, seq, d), BF16),
        compiler_params=_cparams("parallel", "parallel", "arbitrary"),
        name=body.__name__.strip("_").replace("_kernel", ""),
    )(qs, qa, kb, ka, v)


def _trunk(x, pos0, hg_s, ret_s, conv_s, fox_k, fox_v, fox_lf, W):
    bsz, seq, d = x.shape
    m = bsz * seq
    past = fox_k.shape[1]
    tm = min(seq, 256)
    tm_proj = min(seq, 512)
    tm_ffn = min(m, 512)
    x = x.reshape(m, d)

    def ffn(x, a, w_out, layer):
        return _outproj_ffn(x, a, w_out, W["norm_ffn"][layer], W["ffn_w_g"][layer],
                            W["ffn_w_u"][layer], W["ffn_w_d"][layer], tm_ffn, FFN_BOUNDS)

    proj = _norm_proj(x, W["norm_mix"][0], W["hg_w_in"], tm_proj, 512)
    a, hg_s = _hgrn_core(proj, W["hg_lower_bounds"], W["hg_out_gain"], hg_s, 0, bsz, seq, tm)
    x = ffn(x, a, W["hg_w_out"], 0)

    proj = _norm_proj(x, W["norm_mix"][1], W["ret_w_in"], tm_proj, 512)
    inv = jnp.power(ROPE_BASE, -jnp.arange(0, RET_DK, 2, dtype=F32) / RET_DK)
    cos, sin = _rope_table(inv, pos0, seq, tm)
    a, ret_s = _ret_core(proj, cos, sin, W["ret_gn_gain"], ret_s, bsz, seq, tm, min(seq, 128))
    x = ffn(x, a, W["ret_w_out"], 1)

    a, conv_s = _conv_core(x, W["norm_mix"][2], W["conv_w_in"], W["conv_w"], conv_s, bsz, seq,
                            tm_proj)
    x = ffn(x, a, W["conv_w_out"], 2)

    qs, kn, kb, v, vb, vt, lf = _fox_prep(x, W["norm_mix"][3], W["fox_w_qkv"], W["fox_w_f"],
                                          W["fox_b_f"], W["fox_q_gain"], W["fox_k_gain"], bsz,
                                          tm_proj)
    blk = min(seq, 512)
    lf3 = lf.reshape(bsz, seq, LANES)
    kb3 = kb.reshape(bsz, seq, d)
    vb3 = vb.reshape(bsz, seq, d)
    if past:
        lf3 = jnp.concatenate([jnp.pad(fox_lf, ((0, 0), (0, 0), (0, LANES - FOX_HEADS))), lf3], axis=1)
        kb3 = jnp.concatenate([fox_k.reshape(bsz, past, d).astype(BF16), kb3], axis=1)
        past_v = fox_v.reshape(bsz, past, d).astype(BF16)
        vb3 = jnp.concatenate([past_v, vb3], axis=1)
        vt = jnp.concatenate([past_v.transpose(0, 2, 1), vt], axis=2)
    shift, bounded = _fox_score_bound(W["fox_q_gain"], W["fox_k_gain"], fox_lf)
    qa, ka = _fox_gate(lf3, shift, min(seq, 256))
    attn = functools.partial(_fox_attn, bsz=bsz, seq=seq, past=past, blk=blk,
                             pblk=min(past, 512) if past else blk)
    q3 = qs.reshape(bsz, seq, d)
    a = lax.cond(bounded,
                 lambda: attn(_fox_attn_bounded_kernel, True, q3, qa, kb3, ka, vt),
                 lambda: attn(_fox_attn_online_kernel, False, q3, qa, kb3, ka, vb3))
    x = ffn(x, a.reshape(m, d), W["fox_w_out"], 3)

    return (x.reshape(bsz, seq, d), hg_s, ret_s, conv_s,
            kn.reshape(bsz, seq, FOX_HEADS, FOX_DH), v.reshape(bsz, seq, FOX_HEADS, FOX_DH),
            lf[:, :FOX_HEADS].reshape(bsz, seq, FOX_HEADS))


def kernel(x_prompt, x_sample, state_hgrn, state_ret, state_conv, cache_fox_k, cache_fox_v, cache_fox_logf, norm_mix, norm_ffn, hg_w_in, hg_lower_bounds, hg_out_gain, hg_w_out, ret_w_in, ret_gn_gain, ret_w_out, conv_w_in, conv_w, conv_w_out, fox_w_in, fox_b_f, fox_q_gain, fox_k_gain, fox_w_out, ffn_w_gu, ffn_w_down):
    d = D_MODEL
    bf = lambda w: w.astype(BF16)
    W = dict(
        norm_mix=norm_mix, norm_ffn=norm_ffn,
        hg_w_in=bf(hg_w_in), hg_lower_bounds=hg_lower_bounds, hg_out_gain=hg_out_gain,
        hg_w_out=bf(hg_w_out),
        ret_w_in=bf(ret_w_in), ret_gn_gain=ret_gn_gain, ret_w_out=bf(ret_w_out),
        conv_w_in=bf(conv_w_in), conv_w=conv_w, conv_w_out=bf(conv_w_out),
        fox_w_qkv=bf(fox_w_in[:, :3 * d]),
        fox_w_f=bf(jnp.pad(fox_w_in[:, 3 * d:], ((0, 0), (0, LANES - FOX_HEADS)))),
        fox_b_f=jnp.pad(fox_b_f, (0, LANES - FOX_HEADS)).reshape(1, LANES),
        fox_q_gain=fox_q_gain, fox_k_gain=fox_k_gain, fox_w_out=bf(fox_w_out),
        ffn_w_g=bf(ffn_w_gu[:, :, :D_FF]), ffn_w_u=bf(ffn_w_gu[:, :, D_FF:]), ffn_w_d=bf(ffn_w_down),
    )
    bsz = x_prompt.shape[0]
    dt = x_prompt.dtype
    (y_p, hg_p, ret_p, conv_p, fk_p, fv_p, flf_p) = _trunk(
        x_prompt, 0,
        jnp.zeros((bsz, HG_HEADS, HG_DK, HG_DV), F32),
        jnp.zeros((bsz, RET_HEADS, RET_DK, RET_DV), F32),
        jnp.zeros((bsz, CONV_WIDTH - 1, d), dt),
        jnp.zeros((bsz, 0, FOX_HEADS, FOX_DH), dt),
        jnp.zeros((bsz, 0, FOX_HEADS, FOX_DH), dt),
        jnp.zeros((bsz, 0, FOX_HEADS), F32), W)
    past = cache_fox_k.shape[1]
    (y_s, hg_s, ret_s, conv_s, fk_s, fv_s, flf_s) = _trunk(
        x_sample, past, state_hgrn, state_ret, state_conv,
        cache_fox_k, cache_fox_v, cache_fox_logf, W)
    return (y_p, y_s, hg_p, hg_s, ret_p, ret_s, conv_p, conv_s,
            fk_p, fv_p, flf_p, fk_s, fv_s, flf_s)
```

```python
import functools
import math

import numpy as np
import jax
import jax.numpy as jnp
from jax import lax
from jax.experimental import pallas as pl
from jax.experimental.pallas import tpu as pltpu

F32 = jnp.float32
BF16 = jnp.bfloat16

D_MODEL = 1024
EPS = 1e-6
HG_HEADS, HG_DK, HG_DV = 8, 128, 128
HG_CHUNK = 64
RET_HEADS, RET_DK, RET_DV = 4, 256, 512
ROPE_BASE = 10000.0
CONV_WIDTH = 3
FOX_HEADS, FOX_DH = 16, 64
D_FF = 2816
MXU_DIM = 256
FFN_BOUNDS = (0, (D_FF // MXU_DIM + 1) // 2 * MXU_DIM, D_FF)
LANES = 128

VMEM_LIMIT = 56 * 1024 * 1024


def _cparams(*sem):
    return pltpu.CompilerParams(dimension_semantics=sem, vmem_limit_bytes=VMEM_LIMIT)


def _resident(shape):
    nd = len(shape)
    return pl.BlockSpec(shape, lambda *_: (0,) * nd, pipeline_mode=pl.Buffered(1))


def _dot(a, b):
    return jnp.dot(a, b, preferred_element_type=F32)


def _dot_nt(a, b):
    return lax.dot_general(a, b, (((1,), (1,)), ((), ())), preferred_element_type=F32)


def _dot_tn(a, b):
    return lax.dot_general(a, b, (((0,), (0,)), ((), ())), preferred_element_type=F32)


def _sigmoid(x):
    return 0.5 * jnp.tanh(0.5 * x) + 0.5


def _rms(x, g):
    return x * lax.rsqrt(jnp.mean(x * x, axis=-1, keepdims=True) + EPS) * g


def _split3(x):
    hi = x.astype(BF16)
    r = x - hi.astype(F32)
    mid = r.astype(BF16)
    lo = (r - mid.astype(F32)).astype(BF16)
    return hi, mid, lo


def _tri(n):
    r = lax.broadcasted_iota(jnp.int32, (n, n), 0)
    c = lax.broadcasted_iota(jnp.int32, (n, n), 1)
    return r >= c


def _cumsum_rows(tri_bf16, x):
    hi, mid, lo = _split3(x)
    return _dot(tri_bf16, hi) + _dot(tri_bf16, mid) + _dot(tri_bf16, lo)


def _norm_proj_kernel(x_ref, g_ref, w_ref, o_ref, *, tn):
    h = _rms(x_ref[...], g_ref[...]).astype(BF16)
    for c in range(o_ref.shape[1] // tn):
        o_ref[:, c * tn:(c + 1) * tn] = _dot(h, w_ref[:, c * tn:(c + 1) * tn])


def _norm_proj(x, g, w, tm, tn):
    m, d = x.shape
    n = w.shape[1]
    return pl.pallas_call(
        functools.partial(_norm_proj_kernel, tn=tn),
        grid=(m // tm,),
        in_specs=[pl.BlockSpec((tm, d), lambda i: (i, 0)), _resident((1, d)), _resident((d, n))],
        out_specs=pl.BlockSpec((tm, n), lambda i: (i, 0)),
        out_shape=jax.ShapeDtypeStruct((m, n), F32),
        compiler_params=_cparams("parallel"),
        name="norm_proj",
    )(x, g.reshape(1, d), w)


def _ffn_kernel(x_ref, a_ref, wo_ref, g_ref, wg_ref, wu_ref, wd_ref, y_ref, *, bounds):
    x1 = x_ref[...] + _dot(a_ref[...], wo_ref[...])
    h = _rms(x1, g_ref[...]).astype(BF16)
    acc = x1
    for lo, hi in zip(bounds[:-1], bounds[1:]):
        gate = _dot(h, wg_ref[:, lo:hi])
        up = _dot(h, wu_ref[:, lo:hi])
        act = (gate * _sigmoid(gate) * up).astype(BF16)
        acc = acc + _dot(act, wd_ref[lo:hi, :])
    y_ref[...] = acc


def _outproj_ffn(x, a, w_out, g, w_g, w_u, w_d, tm, bounds):
    m, d = x.shape
    ka = a.shape[1]
    return pl.pallas_call(
        functools.partial(_ffn_kernel, bounds=bounds),
        grid=(m // tm,),
        in_specs=[pl.BlockSpec((tm, d), lambda i: (i, 0)),
                  pl.BlockSpec((tm, ka), lambda i: (i, 0)),
                  _resident((ka, d)), _resident((1, d)),
                  _resident((d, D_FF)), _resident((d, D_FF)), _resident((D_FF, d))],
        out_specs=pl.BlockSpec((tm, d), lambda i: (i, 0)),
        out_shape=jax.ShapeDtypeStruct((m, d), F32),
        compiler_params=_cparams("parallel"),
        name="outproj_ffn",
    )(x, a, w_out, g.reshape(1, d), w_g, w_u, w_d)


def _hgrn_kernel(p_ref, lbw_ref, gain_ref, s0_ref, o_ref, sout_ref, st_ref, *, layer, nchunk):
    t = pl.program_id(1)
    L = HG_CHUNK
    nk = HG_HEADS * HG_DK

    @pl.when(t == 0)
    def _():
        for h in range(HG_HEADS):
            st_ref[h] = s0_ref[0, h].T

    lbw = lbw_ref[...]
    e = jnp.exp(lbw - jnp.max(lbw, axis=0, keepdims=True))
    lb = jnp.sum(e[:layer + 1], axis=0, keepdims=True) / jnp.sum(e, axis=0, keepdims=True)
    gain = gain_ref[...]
    tm = p_ref.shape[0]

    r = lax.broadcasted_iota(jnp.int32, (tm, tm), 0)
    c = lax.broadcasted_iota(jnp.int32, (tm, tm), 1)
    shift = L.bit_length() - 1
    tri = (r >= c) & ((r >> shift) == (c >> shift))
    q = p_ref[:, 0:nk]
    fg = lb + (1.0 - lb) * _sigmoid(p_ref[:, nk:2 * nk])
    k = 1.0 - fg
    G = _cumsum_rows(tri.astype(BF16), jnp.log(fg))

    def per_chunk_row(offset):
        return jnp.concatenate(
            [jnp.broadcast_to(G[i * L + offset:i * L + offset + 1], (L, nk)) for i in range(nchunk)],
            axis=0)

    g_mid = per_chunk_row(L // 2)
    g_last = per_chunk_row(L - 1)
    qa = (q * jnp.exp(G - g_mid)).astype(BF16)
    kb = (k * jnp.exp(g_mid - G)).astype(BF16)
    qs = (q * jnp.exp(G)).astype(BF16)
    kl = (k * jnp.exp(g_last - G)).astype(BF16)
    for h in range(HG_HEADS):
        sl = slice(h * HG_DK, (h + 1) * HG_DK)
        v = p_ref[:, 2 * nk + h * HG_DV:2 * nk + (h + 1) * HG_DV].astype(BF16)
        sc = jnp.where(tri, _dot_nt(qa[:, sl], kb[:, sl]), 0.0).astype(BF16)
        o_intra = _dot(sc, v)
        st = st_ref[h]
        outs = []
        for i in range(nchunk):
            rows = slice(i * L, (i + 1) * L)
            outs.append(o_intra[rows] + _dot_nt(qs[rows, sl], st.astype(BF16)))
            dec = jnp.exp(G[(i + 1) * L - 1:(i + 1) * L, sl])
            st = st * dec + _dot_tn(v[rows], kl[rows, sl])
        st_ref[h] = st
        gate = p_ref[:, 3 * nk + h * HG_DV:3 * nk + (h + 1) * HG_DV]
        o_ref[:, h * HG_DV:(h + 1) * HG_DV] = (
            _rms(jnp.concatenate(outs, axis=0), gain) * (gate * _sigmoid(gate))).astype(BF16)

    @pl.when(t == pl.num_programs(1) - 1)
    def _():
        for h in range(HG_HEADS):
            sout_ref[0, h] = st_ref[h].T


def _hgrn_core(proj, lower_bounds, out_gain, state, layer, bsz, seq, tm):
    nt = seq // tm
    w = proj.shape[1]
    st_shape = (1, HG_HEADS, HG_DK, HG_DV)
    return pl.pallas_call(
        functools.partial(_hgrn_kernel, layer=layer, nchunk=tm // HG_CHUNK),
        grid=(bsz, nt),
        in_specs=[pl.BlockSpec((tm, w), lambda b, t: (b * nt + t, 0)),
                  _resident(lower_bounds.shape), _resident((1, HG_DV)),
                  pl.BlockSpec(st_shape, lambda b, t: (b, 0, 0, 0))],
        out_specs=[pl.BlockSpec((tm, HG_HEADS * HG_DV), lambda b, t: (b * nt + t, 0)),
                   pl.BlockSpec(st_shape, lambda b, t: (b, 0, 0, 0))],
        out_shape=[jax.ShapeDtypeStruct((bsz * seq, HG_HEADS * HG_DV), BF16),
                   jax.ShapeDtypeStruct((bsz,) + st_shape[1:], F32)],
        scratch_shapes=[pltpu.VMEM((HG_HEADS, HG_DV, HG_DK), F32)],
        compiler_params=_cparams("parallel", "arbitrary"),
        name="hgrn_core",
    )(proj, lower_bounds, out_gain.reshape(1, HG_DV), state)


def _rope_table_kernel(inv_ref, cos_ref, sin_ref, *, pos0):
    tb = cos_ref.shape[0]
    pos = (pos0 + pl.program_id(0) * tb
           + lax.broadcasted_iota(jnp.int32, cos_ref.shape, 0)).astype(F32)
    ang = pos * inv_ref[...]
    cos_ref[...] = jnp.cos(ang)
    sin_ref[...] = jnp.sin(ang)


def _rope_table(inv, pos0, seq, tb):
    half = inv.shape[0]
    spec = pl.BlockSpec((tb, half), lambda i: (i, 0))
    return pl.pallas_call(
        functools.partial(_rope_table_kernel, pos0=pos0),
        grid=(seq // tb,),
        in_specs=[_resident((1, half))],
        out_specs=[spec, spec],
        out_shape=[jax.ShapeDtypeStruct((seq, half), F32)] * 2,
        compiler_params=_cparams("parallel"),
        name="rope_table",
    )(inv.reshape(1, half))


def _ret_kernel(p_ref, cos_ref, sin_ref, gn_ref, s0_ref, o_ref, sout_ref, st_ref, *, L, nchunk):
    t = pl.program_id(1)
    nq = RET_HEADS * RET_DK
    nv = RET_HEADS * RET_DV
    half = RET_DK // 2

    @pl.when(t == 0)
    def _():
        st_ref[...] = s0_ref[0]

    ti = lax.broadcasted_iota(jnp.int32, (L, 1), 0).astype(F32)
    diff = (lax.broadcasted_iota(jnp.int32, (L, L), 0)
            - lax.broadcasted_iota(jnp.int32, (L, L), 1)).astype(F32)
    scale = RET_DK ** -0.5

    def chunk(c, carry):
        rows = pl.ds(pl.multiple_of(c * L, L), L)
        cos = cos_ref[rows, :]
        sin = sin_ref[rows, :]
        for h in range(RET_HEADS):
            lg = math.log(1.0 - 2.0 ** (-5.0 - h))
            q1 = p_ref[rows, h * RET_DK:h * RET_DK + half]
            q2 = p_ref[rows, h * RET_DK + half:(h + 1) * RET_DK]
            k1 = p_ref[rows, nq + h * RET_DK:nq + h * RET_DK + half]
            k2 = p_ref[rows, nq + h * RET_DK + half:nq + (h + 1) * RET_DK]
            qr = jnp.concatenate([q1 * cos - q2 * sin, q1 * sin + q2 * cos], axis=1).astype(BF16)
            kr = jnp.concatenate([k1 * cos - k2 * sin, k1 * sin + k2 * cos], axis=1) * scale
            v = p_ref[rows, 2 * nq + h * RET_DV:2 * nq + (h + 1) * RET_DV].astype(BF16)
            decay = jnp.where(diff >= 0, jnp.exp(lg * jnp.maximum(diff, 0.0)), 0.0)
            sc = (_dot_nt(qr, kr.astype(BF16)) * decay).astype(BF16)
            st = st_ref[h]
            o = _dot(sc, v) + _dot(qr, st.astype(BF16)) * jnp.exp(lg * (ti + 1.0))
            kd = (kr * jnp.exp(lg * (L - 1.0 - ti))).astype(BF16)
            st_ref[h] = math.exp(lg * L) * st + _dot_tn(kd, v)
            mu = jnp.mean(o, axis=-1, keepdims=True)
            d = o - mu
            var = jnp.mean(d * d, axis=-1, keepdims=True)
            on = d * lax.rsqrt(var + EPS) * gn_ref[:, h * RET_DV:(h + 1) * RET_DV]
            gate = p_ref[rows, 2 * nq + nv + h * RET_DV:2 * nq + nv + (h + 1) * RET_DV]
            o_ref[rows, h * RET_DV:(h + 1) * RET_DV] = (on * (gate * _sigmoid(gate))).astype(BF16)
        return carry

    lax.fori_loop(0, nchunk, chunk, 0)

    @pl.when(t == pl.num_programs(1) - 1)
    def _():
        sout_ref[0] = st_ref[...]


def _ret_core(proj, cos, sin, gn_gain, state, bsz, seq, tm, chunk):
    nt = seq // tm
    w = proj.shape[1]
    nv = RET_HEADS * RET_DV
    half = RET_DK // 2
    st_shape = (1, RET_HEADS, RET_DK, RET_DV)
    return pl.pallas_call(
        functools.partial(_ret_kernel, L=chunk, nchunk=tm // chunk),
        grid=(bsz, nt),
        in_specs=[pl.BlockSpec((tm, w), lambda b, t: (b * nt + t, 0)),
                  pl.BlockSpec((tm, half), lambda b, t: (t, 0)),
                  pl.BlockSpec((tm, half), lambda b, t: (t, 0)),
                  _resident((1, nv)),
                  pl.BlockSpec(st_shape, lambda b, t: (b, 0, 0, 0))],
        out_specs=[pl.BlockSpec((tm, nv), lambda b, t: (b * nt + t, 0)),
                   pl.BlockSpec(st_shape, lambda b, t: (b, 0, 0, 0))],
        out_shape=[jax.ShapeDtypeStruct((bsz * seq, nv), BF16),
                   jax.ShapeDtypeStruct((bsz,) + st_shape[1:], F32)],
        scratch_shapes=[pltpu.VMEM((RET_HEADS, RET_DK, RET_DV), F32)],
        compiler_params=_cparams("parallel", "arbitrary"),
        name="ret_core",
    )(proj, cos, sin, gn_gain.reshape(1, nv), state)


_CONV_PAD = 8


def _conv_kernel(x_ref, g_ref, w_ref, cw_ref, s0_ref, a_ref, sout_ref, z_ref):
    t = pl.program_id(1)
    tm, d = x_ref.shape
    nc = CONV_WIDTH - 1

    @pl.when(t == 0)
    def _():
        z_ref[_CONV_PAD - nc:_CONV_PAD, :] = s0_ref[0]

    h = _rms(x_ref[...], g_ref[...]).astype(BF16)
    b = _dot(h, w_ref[:, 0:d])
    z_ref[_CONV_PAD:_CONV_PAD + tm, :] = _dot(h, w_ref[:, d:2 * d]) * _dot(h, w_ref[:, 2 * d:3 * d])
    y = cw_ref[0:1, :] * z_ref[_CONV_PAD - nc:_CONV_PAD - nc + tm, :]
    for j in range(1, CONV_WIDTH):
        y = y + cw_ref[j:j + 1, :] * z_ref[_CONV_PAD - nc + j:_CONV_PAD - nc + j + tm, :]
    a_ref[...] = (b * y).astype(BF16)
    last = z_ref[_CONV_PAD + tm - nc:_CONV_PAD + tm, :]
    z_ref[_CONV_PAD - nc:_CONV_PAD, :] = last

    @pl.when(t == pl.num_programs(1) - 1)
    def _():
        sout_ref[0] = last


def _conv_core(x, g, w_in, conv_w, state, bsz, seq, tm):
    nt = seq // tm
    d = x.shape[1]
    nc = CONV_WIDTH - 1
    return pl.pallas_call(
        _conv_kernel,
        grid=(bsz, nt),
        in_specs=[pl.BlockSpec((tm, d), lambda b, t: (b * nt + t, 0)),
                  _resident((1, d)), _resident((d, 3 * d)), _resident((CONV_WIDTH, d)),
                  pl.BlockSpec((1, nc, d), lambda b, t: (b, 0, 0))],
        out_specs=[pl.BlockSpec((tm, d), lambda b, t: (b * nt + t, 0)),
                   pl.BlockSpec((1, nc, d), lambda b, t: (b, 0, 0))],
        out_shape=[jax.ShapeDtypeStruct((bsz * seq, d), BF16),
                   jax.ShapeDtypeStruct((bsz, nc, d), F32)],
        scratch_shapes=[pltpu.VMEM((_CONV_PAD + tm, d), F32)],
        compiler_params=_cparams("parallel", "arbitrary"),
        name="conv_core",
    )(x, g.reshape(1, d), w_in, conv_w, state)


_AUG = 6
LOG2E = 1.4426950408889634


def _fox_prep_kernel(x_ref, g_ref, w_ref, wf_ref, bf_ref, qg_ref, kg_ref, sel_ref, selt_ref,
                     qs_ref, kn_ref, kb_ref, v_ref, vb_ref, vt_ref, lf_ref):
    d = x_ref.shape[1]
    h = _rms(x_ref[...], g_ref[...]).astype(BF16)

    def head_norm(a, gain):
        ms = _dot((a * a).astype(BF16), sel_ref[...]) * (1.0 / FOX_DH)
        rs = lax.rsqrt(ms + EPS)
        hi = rs.astype(BF16)
        lo = (rs - hi.astype(F32)).astype(BF16)
        return a * (_dot(hi, selt_ref[...]) + _dot(lo, selt_ref[...])) * gain

    qn = head_norm(_dot(h, w_ref[:, 0:d]), qg_ref[...])
    qs_ref[...] = (qn * (FOX_DH ** -0.5 * LOG2E)).astype(BF16)
    kn = head_norm(_dot(h, w_ref[:, d:2 * d]), kg_ref[...])
    kn_ref[...] = kn
    kb_ref[...] = kn.astype(BF16)
    v = _dot(h, w_ref[:, 2 * d:3 * d])
    v_ref[...] = v
    vb_ref[...] = v.astype(BF16)
    vt_ref[0] = v.T.astype(BF16)
    fl = _dot(h, wf_ref[...]) + bf_ref[...]
    ls = jnp.minimum(fl, 0.0) - jnp.log(1.0 + jnp.exp(-jnp.abs(fl)))
    lane = lax.broadcasted_iota(jnp.int32, fl.shape, 1)
    lf_ref[...] = jnp.where(lane < FOX_HEADS, ls, 0.0)


def _fox_prep(x, g, w_qkv, w_f, b_f, q_gain, k_gain, bsz, tm):
    m, d = x.shape
    nt = m // bsz // tm
    col = np.arange(d) // FOX_DH
    sel = (col[:, None] == np.arange(LANES)[None, :]).astype(np.float32)
    row = pl.BlockSpec((tm, d), lambda i: (i, 0))
    return pl.pallas_call(
        _fox_prep_kernel,
        grid=(m // tm,),
        in_specs=[row, _resident((1, d)), _resident((d, 3 * d)), _resident((d, LANES)),
                  _resident((1, LANES)), _resident((1, d)), _resident((1, d)),
                  _resident((d, LANES)), _resident((LANES, d))],
        out_specs=[row, row, row, row, row,
                   pl.BlockSpec((1, d, tm), lambda i: (i // nt, 0, i % nt)),
                   pl.BlockSpec((tm, LANES), lambda i: (i, 0))],
        out_shape=[jax.ShapeDtypeStruct((m, d), BF16), jax.ShapeDtypeStruct((m, d), F32),
                   jax.ShapeDtypeStruct((m, d), BF16), jax.ShapeDtypeStruct((m, d), F32),
                   jax.ShapeDtypeStruct((m, d), BF16),
                   jax.ShapeDtypeStruct((bsz, d, m // bsz), BF16),
                   jax.ShapeDtypeStruct((m, LANES), F32)],
        compiler_params=_cparams("parallel"),
        name="fox_prep",
    )(x, g.reshape(1, d), w_qkv, w_f, b_f, jnp.tile(q_gain, FOX_HEADS).reshape(1, d),
      jnp.tile(k_gain, FOX_HEADS).reshape(1, d), jnp.asarray(sel, BF16), jnp.asarray(sel.T, BF16))


def _fox_gate_kernel(lf_ref, shift_ref, pq_ref, pk_ref, cq_ref, ck_ref, qa_ref, ka_ref, carry_ref):
    t = pl.program_id(1)
    cb = lf_ref.shape[1]

    @pl.when(t == 0)
    def _():
        carry_ref[...] = jnp.zeros_like(carry_ref)

    F = _cumsum_rows(_tri(cb).astype(BF16), lf_ref[0]) + carry_ref[...]
    carry_ref[...] = F[cb - 1:cb]
    F2 = F * LOG2E
    qa = cq_ref[...]
    ka = ck_ref[...]
    for j, (pq, pk) in enumerate(zip(_split3(F2 - shift_ref[...]), _split3(F2))):
        qa = qa + _dot(pq, pq_ref[j])
        ka = ka + _dot(pk, pk_ref[j])
    qa_ref[0] = qa.astype(BF16)
    ka_ref[0] = ka.astype(BF16)


def _fox_gate(lf_all, shift):
    bsz, tk, _ = lf_all.shape
    cb = max(c for c in range(64, 769, 64) if tk % c == 0)
    place_q = np.zeros((3, LANES, LANES), np.float32)
    place_k = np.zeros((3, LANES, LANES), np.float32)
    const_q = np.zeros((1, LANES), np.float32)
    const_k = np.zeros((1, LANES), np.float32)
    for h in range(FOX_HEADS):
        for j in range(3):
            place_q[j, h, h * _AUG + j] = 1.0
            place_k[j, h, h * _AUG + 3 + j] = -1.0
            const_q[0, h * _AUG + 3 + j] = 1.0
            const_k[0, h * _AUG + j] = 1.0
    spec = pl.BlockSpec((1, cb, LANES), lambda b, t: (b, t, 0))
    return pl.pallas_call(
        _fox_gate_kernel,
        grid=(bsz, tk // cb),
        in_specs=[spec, _resident((1, LANES)), _resident(place_q.shape), _resident(place_k.shape),
                  _resident(const_q.shape), _resident(const_k.shape)],
        out_specs=[spec, spec],
        out_shape=[jax.ShapeDtypeStruct((bsz, tk, LANES), BF16)] * 2,
        scratch_shapes=[pltpu.VMEM((1, LANES), F32)],
        compiler_params=_cparams("parallel", "arbitrary"),
        name="fox_gate",
    )(lf_all, jnp.broadcast_to(shift.astype(F32), (1, LANES)),
      jnp.asarray(place_q, BF16), jnp.asarray(place_k, BF16),
      jnp.asarray(const_q), jnp.asarray(const_k))


_FOX_MAX_BOUND = 40.0


def _fox_score_bound(q_gain, k_gain, cache_lf):
    gq = jnp.max(jnp.abs(q_gain))
    gk = jnp.max(jnp.abs(k_gain))
    bound = (math.sqrt(FOX_DH) * 1.02 * LOG2E) * gq * gk
    ok = bound <= _FOX_MAX_BOUND
    if cache_lf.size:
        ok = ok & (jnp.max(cache_lf) <= 0.0)
    return jnp.where(ok, bound, 0.0), ok


def _fox_query_operands(q_ref, qa_ref, blk):
    pair = pl.program_id(1)
    lane = lax.broadcasted_iota(jnp.int32, (blk, LANES), 1)
    low = lane < FOX_DH
    q = q_ref[0]
    qa = qa_ref[0]
    qops = []
    for s in range(2):
        lo = (2 * pair + s) * _AUG
        qh = jnp.where(low if s == 0 else ~low, q, jnp.zeros_like(q))
        qah = jnp.where((lane >= lo) & (lane < lo + _AUG), qa, jnp.zeros_like(qa))
        qops.append(jnp.concatenate([qh, qah], axis=1))
    return qops, low


def _fox_key_blocks(step, carry, blk, pblk, past, nq):
    if past:
        carry = lax.fori_loop(
            0, past // pblk,
            lambda j, c: step(pl.multiple_of(j * pblk, pblk), pblk, c, False), carry)
    if nq == 1:
        return step(past, blk, carry, True)
    qi = pl.program_id(2)
    per = 2 if blk <= 512 else 1
    wide = per * blk
    nw = qi // per

    def quad(j, c):
        start = pl.multiple_of(past + j * 2 * wide, wide)
        return step(start + wide, wide, step(start, wide, c, False), False)

    carry = lax.fori_loop(0, nw // 2, quad, carry)
    carry = lax.fori_loop(
        0, nw % 2,
        lambda j, c: step(pl.multiple_of(past + (nw // 2) * 2 * wide, wide), wide, c, False), carry)
    if per == 2:
        carry = lax.fori_loop(
            0, qi % 2, lambda j, c: step(pl.multiple_of(past + (qi - 1) * blk, blk), blk, c, False),
            carry)
    return step(pl.multiple_of(past + qi * blk, blk), blk, carry, True)


def _fox_attn_online_kernel(q_ref, qa_ref, k_ref, ka_ref, v_ref, o_ref, *, blk, pblk, past, nq):
    qops, low = _fox_query_operands(q_ref, qa_ref, blk)
    causal = _tri(blk)

    def step(start, size, carry, masked):
        rows = pl.ds(start, size)
        kop = jnp.concatenate([k_ref[0, rows, :], ka_ref[0, rows, :]], axis=1)
        v = v_ref[0, rows, :]
        out = []
        for s in range(2):
            m, l, acc = carry[s]
            sc = _dot_nt(qops[s], kop)
            if masked:
                sc = jnp.where(causal, sc, -jnp.inf)
            m_new = jnp.maximum(m, jnp.max(sc, axis=-1, keepdims=True))
            alpha = jnp.exp2(m - m_new)
            p = jnp.exp2(sc - m_new)
            l = alpha * l + jnp.sum(p, axis=-1, keepdims=True)
            acc = alpha * acc + _dot(p.astype(BF16), v)
            out.append((m_new, l, acc))
        return tuple(out)

    init = tuple((jnp.full((blk, 1), -jnp.inf, F32), jnp.zeros((blk, 1), F32),
                  jnp.zeros((blk, LANES), F32)) for _ in range(2))
    (_, l0, a0), (_, l1, a1) = _fox_key_blocks(step, init, blk, pblk, past, nq)
    o_ref[0] = jnp.where(low, a0 / l0, a1 / l1).astype(BF16)


def _fox_attn_bounded_kernel(q_ref, qa_ref, k_ref, ka_ref, vt_ref, o_ref, *, blk, pblk, past,
                             nq):
    qops, _ = _fox_query_operands(q_ref, qa_ref, blk)
    qop = jnp.concatenate(qops, axis=0)
    r = lax.broadcasted_iota(jnp.int32, (blk, blk), 0)
    c = lax.broadcasted_iota(jnp.int32, (blk, blk), 1)
    causal = r <= c
    chan = lax.broadcasted_iota(jnp.int32, (LANES, 1), 0) < FOX_DH

    def step(start, size, carry, masked):
        rows = pl.ds(start, size)
        kop = jnp.concatenate([k_ref[0, rows, :], ka_ref[0, rows, :]], axis=1)
        vt = vt_ref[0, :, rows]
        ones = jnp.ones_like(vt)
        sc = _dot_nt(kop, qop)
        s0, s1 = sc[:, :blk], sc[:, blk:]
        if masked:
            s0 = jnp.where(causal, s0, -jnp.inf)
            s1 = jnp.where(causal, s1, -jnp.inf)
        a0, a1 = carry
        return (a0 + _dot(jnp.where(chan, vt, ones), jnp.exp2(s0).astype(BF16)),
                a1 + _dot(jnp.where(chan, ones, vt), jnp.exp2(s1).astype(BF16)))

    zero = jnp.zeros((LANES, blk), F32)
    a0, a1 = _fox_key_blocks(step, (zero, zero), blk, pblk, past, nq)
    out_t = jnp.concatenate([a0[:FOX_DH] / a0[FOX_DH:], a1[FOX_DH:] / a1[:FOX_DH]], axis=0)
    o_ref[0] = out_t.T.astype(BF16)


def _fox_attn(body, transposed_v, qs, qa, kb, ka, v, bsz, seq, past, blk, pblk):
    tk = past + seq
    d = qs.shape[-1]
    pb = past // blk
    qspec = pl.BlockSpec((1, blk, LANES), lambda b, p, i: (b, i, p))
    kspec = pl.BlockSpec((1, tk, LANES), lambda b, p, i: (b, 0, p))
    vspec = pl.BlockSpec((1, LANES, tk), lambda b, p, i: (b, p, 0)) if transposed_v else kspec
    return pl.pallas_call(
        functools.partial(body, blk=blk, pblk=pblk, past=past, nq=seq // blk),
        grid=(bsz, d // LANES, seq // blk),
        in_specs=[qspec,
                  pl.BlockSpec((1, blk, LANES), lambda b, p, i: (b, pb + i, 0)),
                  kspec,
                  pl.BlockSpec((1, tk, LANES), lambda b, p, i: (b, 0, 0)),
                  vspec],
        out_specs=qspec,
        out_shape=jax.ShapeDtypeStruct((bsz, seq, d), BF16),
        compiler_params=_cparams("parallel", "parallel", "arbitrary"),
        name=body.__name__.strip("_").replace("_kernel", ""),
    )(qs, qa, kb, ka, v)


def _trunk(x, pos0, hg_s, ret_s, conv_s, fox_k, fox_v, fox_lf, W):
    bsz, seq, d = x.shape
    m = bsz * seq
    past = fox_k.shape[1]
    tm = min(seq, 256)
    tm_proj = min(seq, 512)
    tm_ffn = min(m, 512)
    x = x.reshape(m, d)

    def ffn(x, a, w_out, layer):
        return _outproj_ffn(x, a, w_out, W["norm_ffn"][layer], W["ffn_w_g"][layer],
                            W["ffn_w_u"][layer], W["ffn_w_d"][layer], tm_ffn, FFN_BOUNDS)

    proj = _norm_proj(x, W["norm_mix"][0], W["hg_w_in"], tm_proj, 512)
    a, hg_s = _hgrn_core(proj, W["hg_lower_bounds"], W["hg_out_gain"], hg_s, 0, bsz, seq, tm)
    x = ffn(x, a, W["hg_w_out"], 0)

    proj = _norm_proj(x, W["norm_mix"][1], W["ret_w_in"], tm_proj, 512)
    inv = jnp.power(ROPE_BASE, -jnp.arange(0, RET_DK, 2, dtype=F32) / RET_DK)
    cos, sin = _rope_table(inv, pos0, seq, tm)
    a, ret_s = _ret_core(proj, cos, sin, W["ret_gn_gain"], ret_s, bsz, seq, tm, min(seq, 128))
    x = ffn(x, a, W["ret_w_out"], 1)

    a, conv_s = _conv_core(x, W["norm_mix"][2], W["conv_w_in"], W["conv_w"], conv_s, bsz, seq,
                            tm_proj)
    x = ffn(x, a, W["conv_w_out"], 2)

    qs, kn, kb, v, vb, vt, lf = _fox_prep(x, W["norm_mix"][3], W["fox_w_qkv"], W["fox_w_f"],
                                          W["fox_b_f"], W["fox_q_gain"], W["fox_k_gain"], bsz,
                                          tm_proj)
    blk = min(seq, 1024)
    lf3 = lf.reshape(bsz, seq, LANES)
    kb3 = kb.reshape(bsz, seq, d)
    vb3 = vb.reshape(bsz, seq, d)
    if past:
        lf3 = jnp.concatenate([jnp.pad(fox_lf, ((0, 0), (0, 0), (0, LANES - FOX_HEADS))), lf3], axis=1)
        kb3 = jnp.concatenate([fox_k.reshape(bsz, past, d).astype(BF16), kb3], axis=1)
        past_v = fox_v.reshape(bsz, past, d).astype(BF16)
        vb3 = jnp.concatenate([past_v, vb3], axis=1)
        vt = jnp.concatenate([past_v.transpose(0, 2, 1), vt], axis=2)
    shift, bounded = _fox_score_bound(W["fox_q_gain"], W["fox_k_gain"], fox_lf)
    qa, ka = _fox_gate(lf3, shift)
    attn = functools.partial(_fox_attn, bsz=bsz, seq=seq, past=past, blk=blk,
                             pblk=min(past, 512) if past else blk)
    q3 = qs.reshape(bsz, seq, d)
    a = lax.cond(bounded,
                 lambda: attn(_fox_attn_bounded_kernel, True, q3, qa, kb3, ka, vt),
                 lambda: attn(_fox_attn_online_kernel, False, q3, qa, kb3, ka, vb3))
    x = ffn(x, a.reshape(m, d), W["fox_w_out"], 3)

    return (x.reshape(bsz, seq, d), hg_s, ret_s, conv_s,
            kn.reshape(bsz, seq, FOX_HEADS, FOX_DH), v.reshape(bsz, seq, FOX_HEADS, FOX_DH),
            lf[:, :FOX_HEADS].reshape(bsz, seq, FOX_HEADS))


def kernel(x_prompt, x_sample, state_hgrn, state_ret, state_conv, cache_fox_k, cache_fox_v, cache_fox_logf, norm_mix, norm_ffn, hg_w_in, hg_lower_bounds, hg_out_gain, hg_w_out, ret_w_in, ret_gn_gain, ret_w_out, conv_w_in, conv_w, conv_w_out, fox_w_in, fox_b_f, fox_q_gain, fox_k_gain, fox_w_out, ffn_w_gu, ffn_w_down):
    d = D_MODEL
    bf = lambda w: w.astype(BF16)
    W = dict(
        norm_mix=norm_mix, norm_ffn=norm_ffn,
        hg_w_in=bf(hg_w_in), hg_lower_bounds=hg_lower_bounds, hg_out_gain=hg_out_gain,
        hg_w_out=bf(hg_w_out),
        ret_w_in=bf(ret_w_in), ret_gn_gain=ret_gn_gain, ret_w_out=bf(ret_w_out),
        conv_w_in=bf(conv_w_in), conv_w=conv_w, conv_w_out=bf(conv_w_out),
        fox_w_qkv=bf(fox_w_in[:, :3 * d]),
        fox_w_f=bf(jnp.pad(fox_w_in[:, 3 * d:], ((0, 0), (0, LANES - FOX_HEADS)))),
        fox_b_f=jnp.pad(fox_b_f, (0, LANES - FOX_HEADS)).reshape(1, LANES),
        fox_q_gain=fox_q_gain, fox_k_gain=fox_k_gain, fox_w_out=bf(fox_w_out),
        ffn_w_g=bf(ffn_w_gu[:, :, :D_FF]), ffn_w_u=bf(ffn_w_gu[:, :, D_FF:]), ffn_w_d=bf(ffn_w_down),
    )
    bsz = x_prompt.shape[0]
    dt = x_prompt.dtype
    (y_p, hg_p, ret_p, conv_p, fk_p, fv_p, flf_p) = _trunk(
        x_prompt, 0,
        jnp.zeros((bsz, HG_HEADS, HG_DK, HG_DV), F32),
        jnp.zeros((bsz, RET_HEADS, RET_DK, RET_DV), F32),
        jnp.zeros((bsz, CONV_WIDTH - 1, d), dt),
        jnp.zeros((bsz, 0, FOX_HEADS, FOX_DH), dt),
        jnp.zeros((bsz, 0, FOX_HEADS, FOX_DH), dt),
        jnp.zeros((bsz, 0, FOX_HEADS), F32), W)
    past = cache_fox_k.shape[1]
    (y_s, hg_s, ret_s, conv_s, fk_s, fv_s, flf_s) = _trunk(
        x_sample, past, state_hgrn, state_ret, state_conv,
        cache_fox_k, cache_fox_v, cache_fox_logf, W)
    return (y_p, y_s, hg_p, hg_s, ret_p, ret_s, conv_p, conv_s,
            fk_p, fv_p, flf_p, fk_s, fv_s, flf_s)
```

```python
import functools
import math

import numpy as np
import jax
import jax.numpy as jnp
from jax import lax
from jax.experimental import pallas as pl
from jax.experimental.pallas import tpu as pltpu

F32 = jnp.float32
BF16 = jnp.bfloat16

D_MODEL = 1024
EPS = 1e-6
HG_HEADS, HG_DK, HG_DV = 8, 128, 128
HG_CHUNK = 64
RET_HEADS, RET_DK, RET_DV = 4, 256, 512
ROPE_BASE = 10000.0
CONV_WIDTH = 3
FOX_HEADS, FOX_DH = 16, 64
D_FF = 2816
MXU_DIM = 256
FFN_BOUNDS = (0, (D_FF // MXU_DIM + 1) // 2 * MXU_DIM, D_FF)
LANES = 128

VMEM_LIMIT = 56 * 1024 * 1024


def _cparams(*sem):
    return pltpu.CompilerParams(dimension_semantics=sem, vmem_limit_bytes=VMEM_LIMIT)


def _resident(shape):
    nd = len(shape)
    return pl.BlockSpec(shape, lambda *_: (0,) * nd, pipeline_mode=pl.Buffered(1))


def _dot(a, b):
    return jnp.dot(a, b, preferred_element_type=F32)


def _dot_nt(a, b):
    return lax.dot_general(a, b, (((1,), (1,)), ((), ())), preferred_element_type=F32)


def _dot_tn(a, b):
    return lax.dot_general(a, b, (((0,), (0,)), ((), ())), preferred_element_type=F32)


def _sigmoid(x):
    return 0.5 * jnp.tanh(0.5 * x) + 0.5


def _rms(x, g):
    return x * lax.rsqrt(jnp.mean(x * x, axis=-1, keepdims=True) + EPS) * g


def _split3(x):
    hi = x.astype(BF16)
    r = x - hi.astype(F32)
    mid = r.astype(BF16)
    lo = (r - mid.astype(F32)).astype(BF16)
    return hi, mid, lo


def _tri(n):
    r = lax.broadcasted_iota(jnp.int32, (n, n), 0)
    c = lax.broadcasted_iota(jnp.int32, (n, n), 1)
    return r >= c


def _cumsum_rows(tri_bf16, x):
    hi, mid, lo = _split3(x)
    return _dot(tri_bf16, hi) + _dot(tri_bf16, mid) + _dot(tri_bf16, lo)


def _norm_proj_kernel(x_ref, g_ref, w_ref, o_ref, *, tn):
    h = _rms(x_ref[...], g_ref[...]).astype(BF16)
    for c in range(o_ref.shape[1] // tn):
        o_ref[:, c * tn:(c + 1) * tn] = _dot(h, w_ref[:, c * tn:(c + 1) * tn])


def _norm_proj(x, g, w, tm, tn):
    m, d = x.shape
    n = w.shape[1]
    return pl.pallas_call(
        functools.partial(_norm_proj_kernel, tn=tn),
        grid=(m // tm,),
        in_specs=[pl.BlockSpec((tm, d), lambda i: (i, 0)), _resident((1, d)), _resident((d, n))],
        out_specs=pl.BlockSpec((tm, n), lambda i: (i, 0)),
        out_shape=jax.ShapeDtypeStruct((m, n), F32),
        compiler_params=_cparams("parallel"),
        name="norm_proj",
    )(x, g.reshape(1, d), w)


def _ffn_kernel(x_ref, a_ref, wo_ref, g_ref, wg_ref, wu_ref, wd_ref, y_ref, *, bounds):
    x1 = x_ref[...] + _dot(a_ref[...], wo_ref[...])
    h = _rms(x1, g_ref[...]).astype(BF16)
    acc = x1
    for lo, hi in zip(bounds[:-1], bounds[1:]):
        gate = _dot(h, wg_ref[:, lo:hi])
        up = _dot(h, wu_ref[:, lo:hi])
        act = (gate * _sigmoid(gate) * up).astype(BF16)
        acc = acc + _dot(act, wd_ref[lo:hi, :])
    y_ref[...] = acc


def _outproj_ffn(x, a, w_out, g, w_g, w_u, w_d, tm, bounds):
    m, d = x.shape
    ka = a.shape[1]
    return pl.pallas_call(
        functools.partial(_ffn_kernel, bounds=bounds),
        grid=(m // tm,),
        in_specs=[pl.BlockSpec((tm, d), lambda i: (i, 0)),
                  pl.BlockSpec((tm, ka), lambda i: (i, 0)),
                  _resident((ka, d)), _resident((1, d)),
                  _resident((d, D_FF)), _resident((d, D_FF)), _resident((D_FF, d))],
        out_specs=pl.BlockSpec((tm, d), lambda i: (i, 0)),
        out_shape=jax.ShapeDtypeStruct((m, d), F32),
        compiler_params=_cparams("parallel"),
        name="outproj_ffn",
    )(x, a, w_out, g.reshape(1, d), w_g, w_u, w_d)


def _hgrn_kernel(p_ref, lbw_ref, gain_ref, s0_ref, o_ref, sout_ref, st_ref, *, layer, nchunk):
    t = pl.program_id(1)
    L = HG_CHUNK
    nk = HG_HEADS * HG_DK

    @pl.when(t == 0)
    def _():
        for h in range(HG_HEADS):
            st_ref[h] = s0_ref[0, h].T

    lbw = lbw_ref[...]
    e = jnp.exp(lbw - jnp.max(lbw, axis=0, keepdims=True))
    lb = jnp.sum(e[:layer + 1], axis=0, keepdims=True) / jnp.sum(e, axis=0, keepdims=True)
    gain = gain_ref[...]
    tm = p_ref.shape[0]

    r = lax.broadcasted_iota(jnp.int32, (tm, tm), 0)
    c = lax.broadcasted_iota(jnp.int32, (tm, tm), 1)
    shift = L.bit_length() - 1
    tri = (r >= c) & ((r >> shift) == (c >> shift))
    q = p_ref[:, 0:nk]
    fg = lb + (1.0 - lb) * _sigmoid(p_ref[:, nk:2 * nk])
    k = 1.0 - fg
    G = _cumsum_rows(tri.astype(BF16), jnp.log(fg))

    def per_chunk_row(offset):
        return jnp.concatenate(
            [jnp.broadcast_to(G[i * L + offset:i * L + offset + 1], (L, nk)) for i in range(nchunk)],
            axis=0)

    g_mid = per_chunk_row(L // 2)
    g_last = per_chunk_row(L - 1)
    qa = (q * jnp.exp(G - g_mid)).astype(BF16)
    kb = (k * jnp.exp(g_mid - G)).astype(BF16)
    qs = (q * jnp.exp(G)).astype(BF16)
    kl = (k * jnp.exp(g_last - G)).astype(BF16)
    heads = range(HG_HEADS)
    sls = [slice(h * HG_DK, (h + 1) * HG_DK) for h in heads]
    vs = [p_ref[:, 2 * nk + h * HG_DV:2 * nk + (h + 1) * HG_DV].astype(BF16) for h in heads]
    scs = [jnp.where(tri, _dot_nt(qa[:, sl], kb[:, sl]), 0.0).astype(BF16) for sl in sls]
    o_intra = [_dot(sc, v) for sc, v in zip(scs, vs)]
    us = [[_dot_tn(v[i * L:(i + 1) * L], kl[i * L:(i + 1) * L, sl]) for i in range(nchunk)]
          for v, sl in zip(vs, sls)]
    sts = [st_ref[h] for h in heads]
    outs = [[] for _ in heads]
    for i in range(nchunk):
        rows = slice(i * L, (i + 1) * L)
        for h in heads:
            outs[h].append(o_intra[h][rows] + _dot_nt(qs[rows, sls[h]], sts[h].astype(BF16)))
            sts[h] = sts[h] * jnp.exp(G[(i + 1) * L - 1:(i + 1) * L, sls[h]]) + us[h][i]
    for h in heads:
        st_ref[h] = sts[h]
        gate = p_ref[:, 3 * nk + h * HG_DV:3 * nk + (h + 1) * HG_DV]
        o_ref[:, h * HG_DV:(h + 1) * HG_DV] = (
            _rms(jnp.concatenate(outs[h], axis=0), gain) * (gate * _sigmoid(gate))).astype(BF16)

    @pl.when(t == pl.num_programs(1) - 1)
    def _():
        for h in range(HG_HEADS):
            sout_ref[0, h] = st_ref[h].T


def _hgrn_core(proj, lower_bounds, out_gain, state, layer, bsz, seq, tm):
    nt = seq // tm
    w = proj.shape[1]
    st_shape = (1, HG_HEADS, HG_DK, HG_DV)
    return pl.pallas_call(
        functools.partial(_hgrn_kernel, layer=layer, nchunk=tm // HG_CHUNK),
        grid=(bsz, nt),
        in_specs=[pl.BlockSpec((tm, w), lambda b, t: (b * nt + t, 0)),
                  _resident(lower_bounds.shape), _resident((1, HG_DV)),
                  pl.BlockSpec(st_shape, lambda b, t: (b, 0, 0, 0))],
        out_specs=[pl.BlockSpec((tm, HG_HEADS * HG_DV), lambda b, t: (b * nt + t, 0)),
                   pl.BlockSpec(st_shape, lambda b, t: (b, 0, 0, 0))],
        out_shape=[jax.ShapeDtypeStruct((bsz * seq, HG_HEADS * HG_DV), BF16),
                   jax.ShapeDtypeStruct((bsz,) + st_shape[1:], F32)],
        scratch_shapes=[pltpu.VMEM((HG_HEADS, HG_DV, HG_DK), F32)],
        compiler_params=_cparams("parallel", "arbitrary"),
        name="hgrn_core",
    )(proj, lower_bounds, out_gain.reshape(1, HG_DV), state)


def _rope_table_kernel(inv_ref, cos_ref, sin_ref, *, pos0):
    tb = cos_ref.shape[0]
    pos = (pos0 + pl.program_id(0) * tb
           + lax.broadcasted_iota(jnp.int32, cos_ref.shape, 0)).astype(F32)
    ang = pos * inv_ref[...]
    cos_ref[...] = jnp.cos(ang)
    sin_ref[...] = jnp.sin(ang)


def _rope_table(inv, pos0, seq, tb):
    half = inv.shape[0]
    spec = pl.BlockSpec((tb, half), lambda i: (i, 0))
    return pl.pallas_call(
        functools.partial(_rope_table_kernel, pos0=pos0),
        grid=(seq // tb,),
        in_specs=[_resident((1, half))],
        out_specs=[spec, spec],
        out_shape=[jax.ShapeDtypeStruct((seq, half), F32)] * 2,
        compiler_params=_cparams("parallel"),
        name="rope_table",
    )(inv.reshape(1, half))


def _ret_kernel(p_ref, cos_ref, sin_ref, gn_ref, s0_ref, o_ref, sout_ref, st_ref, *, L, nchunk):
    t = pl.program_id(1)
    nq = RET_HEADS * RET_DK
    nv = RET_HEADS * RET_DV
    half = RET_DK // 2

    @pl.when(t == 0)
    def _():
        st_ref[...] = s0_ref[0]

    ti = lax.broadcasted_iota(jnp.int32, (L, 1), 0).astype(F32)
    diff = (lax.broadcasted_iota(jnp.int32, (L, L), 0)
            - lax.broadcasted_iota(jnp.int32, (L, L), 1)).astype(F32)
    scale = RET_DK ** -0.5

    def chunk(c, carry):
        rows = pl.ds(pl.multiple_of(c * L, L), L)
        cos = cos_ref[rows, :]
        sin = sin_ref[rows, :]
        heads = range(RET_HEADS)
        lgs = [math.log(1.0 - 2.0 ** (-5.0 - h)) for h in heads]
        qrs, krs, vs = [], [], []
        for h in heads:
            q1 = p_ref[rows, h * RET_DK:h * RET_DK + half]
            q2 = p_ref[rows, h * RET_DK + half:(h + 1) * RET_DK]
            k1 = p_ref[rows, nq + h * RET_DK:nq + h * RET_DK + half]
            k2 = p_ref[rows, nq + h * RET_DK + half:nq + (h + 1) * RET_DK]
            qrs.append(jnp.concatenate([q1 * cos - q2 * sin, q1 * sin + q2 * cos],
                                       axis=1).astype(BF16))
            krs.append(jnp.concatenate([k1 * cos - k2 * sin, k1 * sin + k2 * cos], axis=1) * scale)
            vs.append(p_ref[rows, 2 * nq + h * RET_DV:2 * nq + (h + 1) * RET_DV].astype(BF16))
        scs = [(_dot_nt(qrs[h], krs[h].astype(BF16))
                * jnp.where(diff >= 0, jnp.exp(lgs[h] * jnp.maximum(diff, 0.0)), 0.0)).astype(BF16)
               for h in heads]
        sts = [st_ref[h] for h in heads]
        os = [_dot(scs[h], vs[h]) + _dot(qrs[h], sts[h].astype(BF16)) * jnp.exp(lgs[h] * (ti + 1.0))
              for h in heads]
        for h in heads:
            kd = (krs[h] * jnp.exp(lgs[h] * (L - 1.0 - ti))).astype(BF16)
            st_ref[h] = math.exp(lgs[h] * L) * sts[h] + _dot_tn(kd, vs[h])
        for h in heads:
            o = os[h]
            mu = jnp.mean(o, axis=-1, keepdims=True)
            d = o - mu
            var = jnp.mean(d * d, axis=-1, keepdims=True)
            on = d * lax.rsqrt(var + EPS) * gn_ref[:, h * RET_DV:(h + 1) * RET_DV]
            gate = p_ref[rows, 2 * nq + nv + h * RET_DV:2 * nq + nv + (h + 1) * RET_DV]
            o_ref[rows, h * RET_DV:(h + 1) * RET_DV] = (on * (gate * _sigmoid(gate))).astype(BF16)
        return carry

    lax.fori_loop(0, nchunk, chunk, 0)

    @pl.when(t == pl.num_programs(1) - 1)
    def _():
        sout_ref[0] = st_ref[...]


def _ret_core(proj, cos, sin, gn_gain, state, bsz, seq, tm, chunk):
    nt = seq // tm
    w = proj.shape[1]
    nv = RET_HEADS * RET_DV
    half = RET_DK // 2
    st_shape = (1, RET_HEADS, RET_DK, RET_DV)
    return pl.pallas_call(
        functools.partial(_ret_kernel, L=chunk, nchunk=tm // chunk),
        grid=(bsz, nt),
        in_specs=[pl.BlockSpec((tm, w), lambda b, t: (b * nt + t, 0)),
                  pl.BlockSpec((tm, half), lambda b, t: (t, 0)),
                  pl.BlockSpec((tm, half), lambda b, t: (t, 0)),
                  _resident((1, nv)),
                  pl.BlockSpec(st_shape, lambda b, t: (b, 0, 0, 0))],
        out_specs=[pl.BlockSpec((tm, nv), lambda b, t: (b * nt + t, 0)),
                   pl.BlockSpec(st_shape, lambda b, t: (b, 0, 0, 0))],
        out_shape=[jax.ShapeDtypeStruct((bsz * seq, nv), BF16),
                   jax.ShapeDtypeStruct((bsz,) + st_shape[1:], F32)],
        scratch_shapes=[pltpu.VMEM((RET_HEADS, RET_DK, RET_DV), F32)],
        compiler_params=_cparams("parallel", "arbitrary"),
        name="ret_core",
    )(proj, cos, sin, gn_gain.reshape(1, nv), state)


_CONV_PAD = 8


def _conv_kernel(x_ref, g_ref, w_ref, cw_ref, s0_ref, a_ref, sout_ref, z_ref):
    t = pl.program_id(1)
    tm, d = x_ref.shape
    nc = CONV_WIDTH - 1

    @pl.when(t == 0)
    def _():
        z_ref[_CONV_PAD - nc:_CONV_PAD, :] = s0_ref[0]

    h = _rms(x_ref[...], g_ref[...]).astype(BF16)
    b = _dot(h, w_ref[:, 0:d])
    z_ref[_CONV_PAD:_CONV_PAD + tm, :] = _dot(h, w_ref[:, d:2 * d]) * _dot(h, w_ref[:, 2 * d:3 * d])
    y = cw_ref[0:1, :] * z_ref[_CONV_PAD - nc:_CONV_PAD - nc + tm, :]
    for j in range(1, CONV_WIDTH):
        y = y + cw_ref[j:j + 1, :] * z_ref[_CONV_PAD - nc + j:_CONV_PAD - nc + j + tm, :]
    a_ref[...] = (b * y).astype(BF16)
    last = z_ref[_CONV_PAD + tm - nc:_CONV_PAD + tm, :]
    z_ref[_CONV_PAD - nc:_CONV_PAD, :] = last

    @pl.when(t == pl.num_programs(1) - 1)
    def _():
        sout_ref[0] = last


def _conv_core(x, g, w_in, conv_w, state, bsz, seq, tm):
    nt = seq // tm
    d = x.shape[1]
    nc = CONV_WIDTH - 1
    return pl.pallas_call(
        _conv_kernel,
        grid=(bsz, nt),
        in_specs=[pl.BlockSpec((tm, d), lambda b, t: (b * nt + t, 0)),
                  _resident((1, d)), _resident((d, 3 * d)), _resident((CONV_WIDTH, d)),
                  pl.BlockSpec((1, nc, d), lambda b, t: (b, 0, 0))],
        out_specs=[pl.BlockSpec((tm, d), lambda b, t: (b * nt + t, 0)),
                   pl.BlockSpec((1, nc, d), lambda b, t: (b, 0, 0))],
        out_shape=[jax.ShapeDtypeStruct((bsz * seq, d), BF16),
                   jax.ShapeDtypeStruct((bsz, nc, d), F32)],
        scratch_shapes=[pltpu.VMEM((_CONV_PAD + tm, d), F32)],
        compiler_params=_cparams("parallel", "arbitrary"),
        name="conv_core",
    )(x, g.reshape(1, d), w_in, conv_w, state)


_AUG = 6
LOG2E = 1.4426950408889634


def _fox_prep_kernel(x_ref, g_ref, w_ref, wf_ref, bf_ref, qg_ref, kg_ref, sel_ref, selt_ref,
                     qs_ref, kb_ref, vb_ref, vt_ref, lf_ref, kn_hbm, v_hbm, kscr, vscr, sems):
    i = pl.program_id(0)
    n = pl.num_programs(0)
    tm, d = x_ref.shape
    slot = i % 2
    h = _rms(x_ref[...], g_ref[...]).astype(BF16)

    def head_norm(a, gain):
        ms = _dot((a * a).astype(BF16), sel_ref[...]) * (1.0 / FOX_DH)
        rs = lax.rsqrt(ms + EPS)
        hi = rs.astype(BF16)
        lo = (rs - hi.astype(F32)).astype(BF16)
        return a * (_dot(hi, selt_ref[...]) + _dot(lo, selt_ref[...])) * gain

    def head_copies(scr, hbm, sem, s, step):
        return [pltpu.make_async_copy(scr.at[s, hd], hbm.at[pl.ds(step * tm, tm), hd, :], sem.at[s])
                for hd in range(FOX_HEADS)]

    def emit_heads(val, scr, hbm, sem):
        @pl.when(i >= 2)
        def _():
            for c in head_copies(scr, hbm, sem, slot, i - 2):
                c.wait()
        for hd in range(FOX_HEADS):
            scr[slot, hd] = val[:, hd * FOX_DH:(hd + 1) * FOX_DH]
        for c in head_copies(scr, hbm, sem, slot, i):
            c.start()

    qn = head_norm(_dot(h, w_ref[:, 0:d]), qg_ref[...])
    qs_ref[...] = (qn * (FOX_DH ** -0.5 * LOG2E)).astype(BF16)
    kn = head_norm(_dot(h, w_ref[:, d:2 * d]), kg_ref[...])
    emit_heads(kn, kscr, kn_hbm, sems.at[0])
    kb_ref[...] = kn.astype(BF16)
    v = _dot(h, w_ref[:, 2 * d:3 * d])
    emit_heads(v, vscr, v_hbm, sems.at[1])
    vb_ref[...] = v.astype(BF16)
    vt_ref[0] = v.T.astype(BF16)
    fl = _dot(h, wf_ref[...]) + bf_ref[...]
    ls = jnp.minimum(fl, 0.0) - jnp.log(1.0 + jnp.exp(-jnp.abs(fl)))
    lane = lax.broadcasted_iota(jnp.int32, fl.shape, 1)
    lf_ref[...] = jnp.where(lane < FOX_HEADS, ls, 0.0)

    @pl.when(i == n - 1)
    def _():
        for scr, hbm, sem in ((kscr, kn_hbm, sems.at[0]), (vscr, v_hbm, sems.at[1])):
            for c in head_copies(scr, hbm, sem, slot, i):
                c.wait()

    @pl.when((i == n - 1) & (i >= 1))
    def _():
        for scr, hbm, sem in ((kscr, kn_hbm, sems.at[0]), (vscr, v_hbm, sems.at[1])):
            for c in head_copies(scr, hbm, sem, 1 - slot, i - 1):
                c.wait()


def _fox_prep(x, g, w_qkv, w_f, b_f, q_gain, k_gain, bsz, tm):
    m, d = x.shape
    nt = m // bsz // tm
    col = np.arange(d) // FOX_DH
    sel = (col[:, None] == np.arange(LANES)[None, :]).astype(np.float32)
    row = pl.BlockSpec((tm, d), lambda i: (i, 0))
    heads = jax.ShapeDtypeStruct((m, FOX_HEADS, FOX_DH), F32)
    return pl.pallas_call(
        _fox_prep_kernel,
        grid=(m // tm,),
        in_specs=[row, _resident((1, d)), _resident((d, 3 * d)), _resident((d, LANES)),
                  _resident((1, LANES)), _resident((1, d)), _resident((1, d)),
                  _resident((d, LANES)), _resident((LANES, d))],
        out_specs=[row, row, row,
                   pl.BlockSpec((1, d, tm), lambda i: (i // nt, 0, i % nt)),
                   pl.BlockSpec((tm, LANES), lambda i: (i, 0)),
                   pl.BlockSpec(memory_space=pl.ANY), pl.BlockSpec(memory_space=pl.ANY)],
        out_shape=[jax.ShapeDtypeStruct((m, d), BF16), jax.ShapeDtypeStruct((m, d), BF16),
                   jax.ShapeDtypeStruct((m, d), BF16),
                   jax.ShapeDtypeStruct((bsz, d, m // bsz), BF16),
                   jax.ShapeDtypeStruct((m, LANES), F32), heads, heads],
        scratch_shapes=[pltpu.VMEM((2, FOX_HEADS, tm, FOX_DH), F32),
                        pltpu.VMEM((2, FOX_HEADS, tm, FOX_DH), F32),
                        pltpu.SemaphoreType.DMA((2, 2))],
        compiler_params=_cparams("arbitrary"),
        name="fox_prep",
    )(x, g.reshape(1, d), w_qkv, w_f, b_f, jnp.tile(q_gain, FOX_HEADS).reshape(1, d),
      jnp.tile(k_gain, FOX_HEADS).reshape(1, d), jnp.asarray(sel, BF16), jnp.asarray(sel.T, BF16))


def _fox_gate_kernel(lf_ref, shift_ref, pq_ref, pk_ref, cq_ref, ck_ref, qa_ref, ka_ref, carry_ref):
    t = pl.program_id(1)
    cb = lf_ref.shape[1]

    @pl.when(t == 0)
    def _():
        carry_ref[...] = jnp.zeros_like(carry_ref)

    F = _cumsum_rows(_tri(cb).astype(BF16), lf_ref[0]) + carry_ref[...]
    carry_ref[...] = F[cb - 1:cb]
    F2 = F * LOG2E
    qa = cq_ref[...]
    ka = ck_ref[...]
    for j, (pq, pk) in enumerate(zip(_split3(F2 - shift_ref[...]), _split3(F2))):
        qa = qa + _dot(pq, pq_ref[j])
        ka = ka + _dot(pk, pk_ref[j])
    qa_ref[0] = qa.astype(BF16)
    ka_ref[0] = ka.astype(BF16)


def _fox_gate(lf_all, shift):
    bsz, tk, _ = lf_all.shape
    cb = max(c for c in range(64, 769, 64) if tk % c == 0)
    place_q = np.zeros((3, LANES, LANES), np.float32)
    place_k = np.zeros((3, LANES, LANES), np.float32)
    const_q = np.zeros((1, LANES), np.float32)
    const_k = np.zeros((1, LANES), np.float32)
    for h in range(FOX_HEADS):
        for j in range(3):
            place_q[j, h, h * _AUG + j] = 1.0
            place_k[j, h, h * _AUG + 3 + j] = -1.0
            const_q[0, h * _AUG + 3 + j] = 1.0
            const_k[0, h * _AUG + j] = 1.0
    spec = pl.BlockSpec((1, cb, LANES), lambda b, t: (b, t, 0))
    return pl.pallas_call(
        _fox_gate_kernel,
        grid=(bsz, tk // cb),
        in_specs=[spec, _resident((1, LANES)), _resident(place_q.shape), _resident(place_k.shape),
                  _resident(const_q.shape), _resident(const_k.shape)],
        out_specs=[spec, spec],
        out_shape=[jax.ShapeDtypeStruct((bsz, tk, LANES), BF16)] * 2,
        scratch_shapes=[pltpu.VMEM((1, LANES), F32)],
        compiler_params=_cparams("parallel", "arbitrary"),
        name="fox_gate",
    )(lf_all, jnp.broadcast_to(shift.astype(F32), (1, LANES)),
      jnp.asarray(place_q, BF16), jnp.asarray(place_k, BF16),
      jnp.asarray(const_q), jnp.asarray(const_k))


_FOX_MAX_BOUND = 40.0


def _fox_score_bound(q_gain, k_gain, cache_lf):
    gq = jnp.max(jnp.abs(q_gain))
    gk = jnp.max(jnp.abs(k_gain))
    bound = (math.sqrt(FOX_DH) * 1.02 * LOG2E) * gq * gk
    ok = bound <= _FOX_MAX_BOUND
    if cache_lf.size:
        ok = ok & (jnp.max(cache_lf) <= 0.0)
    return jnp.where(ok, bound, 0.0), ok


def _fox_query_operands(q_ref, qa_ref, blk):
    pair = pl.program_id(1)
    lane = lax.broadcasted_iota(jnp.int32, (blk, LANES), 1)
    low = lane < FOX_DH
    q = q_ref[0]
    qa = qa_ref[0]
    qops = []
    for s in range(2):
        lo = (2 * pair + s) * _AUG
        qh = jnp.where(low if s == 0 else ~low, q, jnp.zeros_like(q))
        qah = jnp.where((lane >= lo) & (lane < lo + _AUG), qa, jnp.zeros_like(qa))
        qops.append(jnp.concatenate([qh, qah], axis=1))
    return qops, low


def _fox_key_blocks(step, carry, blk, pblk, past, nq):
    if past:
        carry = lax.fori_loop(
            0, past // pblk,
            lambda j, c: step(pl.multiple_of(j * pblk, pblk), pblk, c, False), carry)
    if nq == 1:
        return step(past, blk, carry, True)
    qi = pl.program_id(2)
    per = 2 if blk <= 512 else 1
    wide = per * blk
    nw = qi // per

    def quad(j, c):
        start = pl.multiple_of(past + j * 2 * wide, wide)
        return step(start + wide, wide, step(start, wide, c, False), False)

    carry = lax.fori_loop(0, nw // 2, quad, carry)
    carry = lax.fori_loop(
        0, nw % 2,
        lambda j, c: step(pl.multiple_of(past + (nw // 2) * 2 * wide, wide), wide, c, False), carry)
    if per == 2:
        carry = lax.fori_loop(
            0, qi % 2, lambda j, c: step(pl.multiple_of(past + (qi - 1) * blk, blk), blk, c, False),
            carry)
    return step(pl.multiple_of(past + qi * blk, blk), blk, carry, True)


def _fox_attn_online_kernel(q_ref, qa_ref, k_ref, ka_ref, v_ref, o_ref, *, blk, pblk, past, nq):
    qops, low = _fox_query_operands(q_ref, qa_ref, blk)
    causal = _tri(blk)

    def step(start, size, carry, masked):
        rows = pl.ds(start, size)
        kop = jnp.concatenate([k_ref[0, rows, :], ka_ref[0, rows, :]], axis=1)
        v = v_ref[0, rows, :]
        out = []
        for s in range(2):
            m, l, acc = carry[s]
            sc = _dot_nt(qops[s], kop)
            if masked:
                sc = jnp.where(causal, sc, -jnp.inf)
            m_new = jnp.maximum(m, jnp.max(sc, axis=-1, keepdims=True))
            alpha = jnp.exp2(m - m_new)
            p = jnp.exp2(sc - m_new)
            l = alpha * l + jnp.sum(p, axis=-1, keepdims=True)
            acc = alpha * acc + _dot(p.astype(BF16), v)
            out.append((m_new, l, acc))
        return tuple(out)

    init = tuple((jnp.full((blk, 1), -jnp.inf, F32), jnp.zeros((blk, 1), F32),
                  jnp.zeros((blk, LANES), F32)) for _ in range(2))
    (_, l0, a0), (_, l1, a1) = _fox_key_blocks(step, init, blk, pblk, past, nq)
    o_ref[0] = jnp.where(low, a0 / l0, a1 / l1).astype(BF16)


def _fox_attn_bounded_kernel(q_ref, qa_ref, k_ref, ka_ref, vt_ref, o_ref, *, blk, pblk, past,
                             nq):
    qops, _ = _fox_query_operands(q_ref, qa_ref, blk)
    qop = jnp.concatenate(qops, axis=0)
    r = lax.broadcasted_iota(jnp.int32, (blk, blk), 0)
    c = lax.broadcasted_iota(jnp.int32, (blk, blk), 1)
    causal = r <= c
    chan = lax.broadcasted_iota(jnp.int32, (LANES, 1), 0) < FOX_DH

    def step(start, size, carry, masked):
        rows = pl.ds(start, size)
        kop = jnp.concatenate([k_ref[0, rows, :], ka_ref[0, rows, :]], axis=1)
        vt = vt_ref[0, :, rows]
        ones = jnp.ones_like(vt)
        sc = _dot_nt(kop, qop)
        s0, s1 = sc[:, :blk], sc[:, blk:]
        if masked:
            s0 = jnp.where(causal, s0, -jnp.inf)
            s1 = jnp.where(causal, s1, -jnp.inf)
        a0, a1 = carry
        return (a0 + _dot(jnp.where(chan, vt, ones), jnp.exp2(s0).astype(BF16)),
                a1 + _dot(jnp.where(chan, ones, vt), jnp.exp2(s1).astype(BF16)))

    zero = jnp.zeros((LANES, blk), F32)
    a0, a1 = _fox_key_blocks(step, (zero, zero), blk, pblk, past, nq)
    out_t = jnp.concatenate([a0[:FOX_DH] / a0[FOX_DH:], a1[FOX_DH:] / a1[:FOX_DH]], axis=0)
    o_ref[0] = out_t.T.astype(BF16)


def _fox_attn(body, transposed_v, qs, qa, kb, ka, v, bsz, seq, past, blk, pblk):
    tk = past + seq
    d = qs.shape[-1]
    pb = past // blk
    qspec = pl.BlockSpec((1, blk, LANES), lambda b, p, i: (b, i, p))
    kspec = pl.BlockSpec((1, tk, LANES), lambda b, p, i: (b, 0, p))
    vspec = pl.BlockSpec((1, LANES, tk), lambda b, p, i: (b, p, 0)) if transposed_v else kspec
    return pl.pallas_call(
        functools.partial(body, blk=blk, pblk=pblk, past=past, nq=seq // blk),
        grid=(bsz, d // LANES, seq // blk),
        in_specs=[qspec,
                  pl.BlockSpec((1, blk, LANES), lambda b, p, i: (b, pb + i, 0)),
                  kspec,
                  pl.BlockSpec((1, tk, LANES), lambda b, p, i: (b, 0, 0)),
                  vspec],
        out_specs=qspec,
        out_shape=jax.ShapeDtypeStruct((bsz, seq, d), BF16),
        compiler_params=_cparams("parallel", "parallel", "arbitrary"),
        name=body.__name__.strip("_").replace("_kernel", ""),
    )(qs, qa, kb, ka, v)


def _trunk(x, pos0, hg_s, ret_s, conv_s, fox_k, fox_v, fox_lf, W):
    bsz, seq, d = x.shape
    m = bsz * seq
    past = fox_k.shape[1]
    tm = min(seq, 256)
    tm_proj = min(seq, 512)
    tm_ffn = min(m, 512)
    x = x.reshape(m, d)

    def ffn(x, a, w_out, layer):
        return _outproj_ffn(x, a, w_out, W["norm_ffn"][layer], W["ffn_w_g"][layer],
                            W["ffn_w_u"][layer], W["ffn_w_d"][layer], tm_ffn, FFN_BOUNDS)

    proj = _norm_proj(x, W["norm_mix"][0], W["hg_w_in"], tm_proj, 512)
    a, hg_s = _hgrn_core(proj, W["hg_lower_bounds"], W["hg_out_gain"], hg_s, 0, bsz, seq, tm)
    x = ffn(x, a, W["hg_w_out"], 0)

    proj = _norm_proj(x, W["norm_mix"][1], W["ret_w_in"], tm_proj, 512)
    inv = jnp.power(ROPE_BASE, -jnp.arange(0, RET_DK, 2, dtype=F32) / RET_DK)
    cos, sin = _rope_table(inv, pos0, seq, tm)
    a, ret_s = _ret_core(proj, cos, sin, W["ret_gn_gain"], ret_s, bsz, seq, tm, min(seq, 256))
    x = ffn(x, a, W["ret_w_out"], 1)

    a, conv_s = _conv_core(x, W["norm_mix"][2], W["conv_w_in"], W["conv_w"], conv_s, bsz, seq,
                            tm_proj)
    x = ffn(x, a, W["conv_w_out"], 2)

    qs, kb, vb, vt, lf, kn, v = _fox_prep(x, W["norm_mix"][3], W["fox_w_qkv"], W["fox_w_f"],
                                          W["fox_b_f"], W["fox_q_gain"], W["fox_k_gain"], bsz,
                                          tm_proj)
    blk = min(seq, 1024)
    lf3 = lf.reshape(bsz, seq, LANES)
    kb3 = kb.reshape(bsz, seq, d)
    vb3 = vb.reshape(bsz, seq, d)
    if past:
        lf3 = jnp.concatenate([jnp.pad(fox_lf, ((0, 0), (0, 0), (0, LANES - FOX_HEADS))), lf3], axis=1)
        kb3 = jnp.concatenate([fox_k.reshape(bsz, past, d).astype(BF16), kb3], axis=1)
        past_v = fox_v.reshape(bsz, past, d).astype(BF16)
        vb3 = jnp.concatenate([past_v, vb3], axis=1)
        vt = jnp.concatenate([past_v.transpose(0, 2, 1), vt], axis=2)
    shift, bounded = _fox_score_bound(W["fox_q_gain"], W["fox_k_gain"], fox_lf)
    qa, ka = _fox_gate(lf3, shift)
    attn = functools.partial(_fox_attn, bsz=bsz, seq=seq, past=past, blk=blk,
                             pblk=min(past, 512) if past else blk)
    q3 = qs.reshape(bsz, seq, d)
    a = lax.cond(bounded,
                 lambda: attn(_fox_attn_bounded_kernel, True, q3, qa, kb3, ka, vt),
                 lambda: attn(_fox_attn_online_kernel, False, q3, qa, kb3, ka, vb3))
    x = ffn(x, a.reshape(m, d), W["fox_w_out"], 3)

    return (x.reshape(bsz, seq, d), hg_s, ret_s, conv_s,
            kn.reshape(bsz, seq, FOX_HEADS, FOX_DH), v.reshape(bsz, seq, FOX_HEADS, FOX_DH),
            lf[:, :FOX_HEADS].reshape(bsz, seq, FOX_HEADS))


def kernel(x_prompt, x_sample, state_hgrn, state_ret, state_conv, cache_fox_k, cache_fox_v, cache_fox_logf, norm_mix, norm_ffn, hg_w_in, hg_lower_bounds, hg_out_gain, hg_w_out, ret_w_in, ret_gn_gain, ret_w_out, conv_w_in, conv_w, conv_w_out, fox_w_in, fox_b_f, fox_q_gain, fox_k_gain, fox_w_out, ffn_w_gu, ffn_w_down):
    d = D_MODEL
    bf = lambda w: w.astype(BF16)
    W = dict(
        norm_mix=norm_mix, norm_ffn=norm_ffn,
        hg_w_in=bf(hg_w_in), hg_lower_bounds=hg_lower_bounds, hg_out_gain=hg_out_gain,
        hg_w_out=bf(hg_w_out),
        ret_w_in=bf(ret_w_in), ret_gn_gain=ret_gn_gain, ret_w_out=bf(ret_w_out),
        conv_w_in=bf(conv_w_in), conv_w=conv_w, conv_w_out=bf(conv_w_out),
        fox_w_qkv=bf(fox_w_in[:, :3 * d]),
        fox_w_f=bf(jnp.pad(fox_w_in[:, 3 * d:], ((0, 0), (0, LANES - FOX_HEADS)))),
        fox_b_f=jnp.pad(fox_b_f, (0, LANES - FOX_HEADS)).reshape(1, LANES),
        fox_q_gain=fox_q_gain, fox_k_gain=fox_k_gain, fox_w_out=bf(fox_w_out),
        ffn_w_g=bf(ffn_w_gu[:, :, :D_FF]), ffn_w_u=bf(ffn_w_gu[:, :, D_FF:]), ffn_w_d=bf(ffn_w_down),
    )
    bsz = x_prompt.shape[0]
    dt = x_prompt.dtype
    (y_p, hg_p, ret_p, conv_p, fk_p, fv_p, flf_p) = _trunk(
        x_prompt, 0,
        jnp.zeros((bsz, HG_HEADS, HG_DK, HG_DV), F32),
        jnp.zeros((bsz, RET_HEADS, RET_DK, RET_DV), F32),
        jnp.zeros((bsz, CONV_WIDTH - 1, d), dt),
        jnp.zeros((bsz, 0, FOX_HEADS, FOX_DH), dt),
        jnp.zeros((bsz, 0, FOX_HEADS, FOX_DH), dt),
        jnp.zeros((bsz, 0, FOX_HEADS), F32), W)
    past = cache_fox_k.shape[1]
    (y_s, hg_s, ret_s, conv_s, fk_s, fv_s, flf_s) = _trunk(
        x_sample, past, state_hgrn, state_ret, state_conv,
        cache_fox_k, cache_fox_v, cache_fox_logf, W)
    return (y_p, y_s, hg_p, hg_s, ret_p, ret_s, conv_p, conv_s,
            fk_p, fv_p, flf_p, fk_s, fv_s, flf_s)
```

```python
import functools
import math

import numpy as np
import jax
import jax.numpy as jnp
from jax import lax
from jax.experimental import pallas as pl
from jax.experimental.pallas import tpu as pltpu

F32 = jnp.float32
BF16 = jnp.bfloat16

D_MODEL = 1024
EPS = 1e-6
HG_HEADS, HG_DK, HG_DV = 8, 128, 128
HG_CHUNK = 64
RET_HEADS, RET_DK, RET_DV = 4, 256, 512
ROPE_BASE = 10000.0
CONV_WIDTH = 3
FOX_HEADS, FOX_DH = 16, 64
D_FF = 2816
MXU_DIM = 256
FFN_BOUNDS = (0, (D_FF // MXU_DIM + 1) // 2 * MXU_DIM, D_FF)
LANES = 128

VMEM_LIMIT = 56 * 1024 * 1024


def _cparams(*sem):
    return pltpu.CompilerParams(dimension_semantics=sem, vmem_limit_bytes=VMEM_LIMIT)


def _resident(shape):
    nd = len(shape)
    return pl.BlockSpec(shape, lambda *_: (0,) * nd, pipeline_mode=pl.Buffered(1))


def _dot(a, b):
    return jnp.dot(a, b, preferred_element_type=F32)


def _dot_nt(a, b):
    return lax.dot_general(a, b, (((1,), (1,)), ((), ())), preferred_element_type=F32)


def _dot_tn(a, b):
    return lax.dot_general(a, b, (((0,), (0,)), ((), ())), preferred_element_type=F32)


def _sigmoid(x):
    return 0.5 * jnp.tanh(0.5 * x) + 0.5


def _rms(x, g):
    return x * lax.rsqrt(jnp.mean(x * x, axis=-1, keepdims=True) + EPS) * g


def _split3(x):
    hi = x.astype(BF16)
    r = x - hi.astype(F32)
    mid = r.astype(BF16)
    lo = (r - mid.astype(F32)).astype(BF16)
    return hi, mid, lo


def _tri(n):
    r = lax.broadcasted_iota(jnp.int32, (n, n), 0)
    c = lax.broadcasted_iota(jnp.int32, (n, n), 1)
    return r >= c


def _cumsum_rows(tri_bf16, x):
    hi, mid, lo = _split3(x)
    return _dot(tri_bf16, hi) + _dot(tri_bf16, mid) + _dot(tri_bf16, lo)


def _norm_proj_kernel(x_ref, g_ref, w_ref, o_ref, *, tn):
    h = _rms(x_ref[...], g_ref[...]).astype(BF16)
    for c in range(o_ref.shape[1] // tn):
        o_ref[:, c * tn:(c + 1) * tn] = _dot(h, w_ref[:, c * tn:(c + 1) * tn])


def _norm_proj(x, g, w, tm, tn):
    m, d = x.shape
    n = w.shape[1]
    return pl.pallas_call(
        functools.partial(_norm_proj_kernel, tn=tn),
        grid=(m // tm,),
        in_specs=[pl.BlockSpec((tm, d), lambda i: (i, 0)), _resident((1, d)), _resident((d, n))],
        out_specs=pl.BlockSpec((tm, n), lambda i: (i, 0)),
        out_shape=jax.ShapeDtypeStruct((m, n), F32),
        compiler_params=_cparams("parallel"),
        name="norm_proj",
    )(x, g.reshape(1, d), w)


def _ffn_kernel(x_ref, a_ref, wo_ref, g_ref, wg_ref, wu_ref, wd_ref, y_ref, *, bounds):
    tm = x_ref.shape[0]
    groups = [slice(0, tm // 2), slice(tm // 2, tm)] if tm % 32 == 0 else [slice(0, tm)]
    x1 = [x_ref[r, :] + _dot(a_ref[r, :], wo_ref[...]) for r in groups]
    h = [_rms(x, g_ref[...]).astype(BF16) for x in x1]
    acc = list(x1)
    for lo, hi in zip(bounds[:-1], bounds[1:]):
        gate = [_dot(hh, wg_ref[:, lo:hi]) for hh in h]
        up = [_dot(hh, wu_ref[:, lo:hi]) for hh in h]
        act = [(g * _sigmoid(g) * u).astype(BF16) for g, u in zip(gate, up)]
        acc = [a + _dot(t, wd_ref[lo:hi, :]) for a, t in zip(acc, act)]
    for r, a in zip(groups, acc):
        y_ref[r, :] = a


def _outproj_ffn(x, a, w_out, g, w_g, w_u, w_d, tm, bounds):
    m, d = x.shape
    ka = a.shape[1]
    return pl.pallas_call(
        functools.partial(_ffn_kernel, bounds=bounds),
        grid=(m // tm,),
        in_specs=[pl.BlockSpec((tm, d), lambda i: (i, 0)),
                  pl.BlockSpec((tm, ka), lambda i: (i, 0)),
                  _resident((ka, d)), _resident((1, d)),
                  _resident((d, D_FF)), _resident((d, D_FF)), _resident((D_FF, d))],
        out_specs=pl.BlockSpec((tm, d), lambda i: (i, 0)),
        out_shape=jax.ShapeDtypeStruct((m, d), F32),
        compiler_params=_cparams("parallel"),
        name="outproj_ffn",
    )(x, a, w_out, g.reshape(1, d), w_g, w_u, w_d)


def _hgrn_kernel(p_ref, lbw_ref, gain_ref, s0_ref, o_ref, sout_ref, st_ref, *, layer, nchunk):
    t = pl.program_id(1)
    L = HG_CHUNK
    nk = HG_HEADS * HG_DK

    @pl.when(t == 0)
    def _():
        for h in range(HG_HEADS):
            st_ref[h] = s0_ref[0, h].T

    lbw = lbw_ref[...]
    e = jnp.exp(lbw - jnp.max(lbw, axis=0, keepdims=True))
    lb = jnp.sum(e[:layer + 1], axis=0, keepdims=True) / jnp.sum(e, axis=0, keepdims=True)
    gain = gain_ref[...]
    tm = p_ref.shape[0]

    r = lax.broadcasted_iota(jnp.int32, (tm, tm), 0)
    c = lax.broadcasted_iota(jnp.int32, (tm, tm), 1)
    shift = L.bit_length() - 1
    tri = (r >= c) & ((r >> shift) == (c >> shift))
    q = p_ref[:, 0:nk]
    fg = lb + (1.0 - lb) * _sigmoid(p_ref[:, nk:2 * nk])
    k = 1.0 - fg
    G = _cumsum_rows(tri.astype(BF16), jnp.log(fg))

    def per_chunk_row(offset):
        return jnp.concatenate(
            [jnp.broadcast_to(G[i * L + offset:i * L + offset + 1], (L, nk)) for i in range(nchunk)],
            axis=0)

    g_mid = per_chunk_row(L // 2)
    g_last = per_chunk_row(L - 1)
    qa = (q * jnp.exp(G - g_mid)).astype(BF16)
    kb = (k * jnp.exp(g_mid - G)).astype(BF16)
    qs = (q * jnp.exp(G)).astype(BF16)
    kl = (k * jnp.exp(g_last - G)).astype(BF16)
    heads = range(HG_HEADS)
    sls = [slice(h * HG_DK, (h + 1) * HG_DK) for h in heads]
    vs = [p_ref[:, 2 * nk + h * HG_DV:2 * nk + (h + 1) * HG_DV].astype(BF16) for h in heads]
    scs = [jnp.where(tri, _dot_nt(qa[:, sl], kb[:, sl]), 0.0).astype(BF16) for sl in sls]
    o_intra = [_dot(sc, v) for sc, v in zip(scs, vs)]
    us = [[_dot_tn(v[i * L:(i + 1) * L], kl[i * L:(i + 1) * L, sl]) for i in range(nchunk)]
          for v, sl in zip(vs, sls)]
    sts = [st_ref[h] for h in heads]
    outs = [[] for _ in heads]
    for i in range(nchunk):
        rows = slice(i * L, (i + 1) * L)
        for h in heads:
            outs[h].append(o_intra[h][rows] + _dot_nt(qs[rows, sls[h]], sts[h].astype(BF16)))
            sts[h] = sts[h] * jnp.exp(G[(i + 1) * L - 1:(i + 1) * L, sls[h]]) + us[h][i]
    for h in heads:
        st_ref[h] = sts[h]
        gate = p_ref[:, 3 * nk + h * HG_DV:3 * nk + (h + 1) * HG_DV]
        o_ref[:, h * HG_DV:(h + 1) * HG_DV] = (
            _rms(jnp.concatenate(outs[h], axis=0), gain) * (gate * _sigmoid(gate))).astype(BF16)

    @pl.when(t == pl.num_programs(1) - 1)
    def _():
        for h in range(HG_HEADS):
            sout_ref[0, h] = st_ref[h].T


def _hgrn_core(proj, lower_bounds, out_gain, state, layer, bsz, seq, tm):
    nt = seq // tm
    w = proj.shape[1]
    st_shape = (1, HG_HEADS, HG_DK, HG_DV)
    return pl.pallas_call(
        functools.partial(_hgrn_kernel, layer=layer, nchunk=tm // HG_CHUNK),
        grid=(bsz, nt),
        in_specs=[pl.BlockSpec((tm, w), lambda b, t: (b * nt + t, 0)),
                  _resident(lower_bounds.shape), _resident((1, HG_DV)),
                  pl.BlockSpec(st_shape, lambda b, t: (b, 0, 0, 0))],
        out_specs=[pl.BlockSpec((tm, HG_HEADS * HG_DV), lambda b, t: (b * nt + t, 0)),
                   pl.BlockSpec(st_shape, lambda b, t: (b, 0, 0, 0))],
        out_shape=[jax.ShapeDtypeStruct((bsz * seq, HG_HEADS * HG_DV), BF16),
                   jax.ShapeDtypeStruct((bsz,) + st_shape[1:], F32)],
        scratch_shapes=[pltpu.VMEM((HG_HEADS, HG_DV, HG_DK), F32)],
        compiler_params=_cparams("parallel", "arbitrary"),
        name="hgrn_core",
    )(proj, lower_bounds, out_gain.reshape(1, HG_DV), state)


def _rope_table_kernel(inv_ref, cos_ref, sin_ref, *, pos0):
    tb = cos_ref.shape[0]
    pos = (pos0 + pl.program_id(0) * tb
           + lax.broadcasted_iota(jnp.int32, cos_ref.shape, 0)).astype(F32)
    ang = pos * inv_ref[...]
    cos_ref[...] = jnp.cos(ang)
    sin_ref[...] = jnp.sin(ang)


def _rope_table(inv, pos0, seq, tb):
    half = inv.shape[0]
    spec = pl.BlockSpec((tb, half), lambda i: (i, 0))
    return pl.pallas_call(
        functools.partial(_rope_table_kernel, pos0=pos0),
        grid=(seq // tb,),
        in_specs=[_resident((1, half))],
        out_specs=[spec, spec],
        out_shape=[jax.ShapeDtypeStruct((seq, half), F32)] * 2,
        compiler_params=_cparams("parallel"),
        name="rope_table",
    )(inv.reshape(1, half))


def _ret_kernel(p_ref, cos_ref, sin_ref, gn_ref, s0_ref, o_ref, sout_ref, st_ref, *, L, nchunk):
    t = pl.program_id(1)
    nq = RET_HEADS * RET_DK
    nv = RET_HEADS * RET_DV
    half = RET_DK // 2

    @pl.when(t == 0)
    def _():
        st_ref[...] = s0_ref[0]

    ti = lax.broadcasted_iota(jnp.int32, (L, 1), 0).astype(F32)
    diff = (lax.broadcasted_iota(jnp.int32, (L, L), 0)
            - lax.broadcasted_iota(jnp.int32, (L, L), 1)).astype(F32)
    scale = RET_DK ** -0.5

    def chunk(c, carry):
        rows = pl.ds(pl.multiple_of(c * L, L), L)
        cos = cos_ref[rows, :]
        sin = sin_ref[rows, :]
        heads = range(RET_HEADS)
        lgs = [math.log(1.0 - 2.0 ** (-5.0 - h)) for h in heads]
        qrs, krs, vs = [], [], []
        for h in heads:
            q1 = p_ref[rows, h * RET_DK:h * RET_DK + half]
            q2 = p_ref[rows, h * RET_DK + half:(h + 1) * RET_DK]
            k1 = p_ref[rows, nq + h * RET_DK:nq + h * RET_DK + half]
            k2 = p_ref[rows, nq + h * RET_DK + half:nq + (h + 1) * RET_DK]
            qrs.append(jnp.concatenate([q1 * cos - q2 * sin, q1 * sin + q2 * cos],
                                       axis=1).astype(BF16))
            krs.append(jnp.concatenate([k1 * cos - k2 * sin, k1 * sin + k2 * cos], axis=1) * scale)
            vs.append(p_ref[rows, 2 * nq + h * RET_DV:2 * nq + (h + 1) * RET_DV].astype(BF16))
        scs = [(_dot_nt(qrs[h], krs[h].astype(BF16))
                * jnp.where(diff >= 0, jnp.exp(lgs[h] * jnp.maximum(diff, 0.0)), 0.0)).astype(BF16)
               for h in heads]
        sts = [st_ref[h] for h in heads]
        os = [_dot(scs[h], vs[h]) + _dot(qrs[h], sts[h].astype(BF16)) * jnp.exp(lgs[h] * (ti + 1.0))
              for h in heads]
        for h in heads:
            kd = (krs[h] * jnp.exp(lgs[h] * (L - 1.0 - ti))).astype(BF16)
            st_ref[h] = math.exp(lgs[h] * L) * sts[h] + _dot_tn(kd, vs[h])
        for h in heads:
            o = os[h]
            mu = jnp.mean(o, axis=-1, keepdims=True)
            d = o - mu
            var = jnp.mean(d * d, axis=-1, keepdims=True)
            on = d * lax.rsqrt(var + EPS) * gn_ref[:, h * RET_DV:(h + 1) * RET_DV]
            gate = p_ref[rows, 2 * nq + nv + h * RET_DV:2 * nq + nv + (h + 1) * RET_DV]
            o_ref[rows, h * RET_DV:(h + 1) * RET_DV] = (on * (gate * _sigmoid(gate))).astype(BF16)
        return carry

    lax.fori_loop(0, nchunk, chunk, 0)

    @pl.when(t == pl.num_programs(1) - 1)
    def _():
        sout_ref[0] = st_ref[...]


def _ret_core(proj, cos, sin, gn_gain, state, bsz, seq, tm, chunk):
    nt = seq // tm
    w = proj.shape[1]
    nv = RET_HEADS * RET_DV
    half = RET_DK // 2
    st_shape = (1, RET_HEADS, RET_DK, RET_DV)
    return pl.pallas_call(
        functools.partial(_ret_kernel, L=chunk, nchunk=tm // chunk),
        grid=(bsz, nt),
        in_specs=[pl.BlockSpec((tm, w), lambda b, t: (b * nt + t, 0)),
                  pl.BlockSpec((tm, half), lambda b, t: (t, 0)),
                  pl.BlockSpec((tm, half), lambda b, t: (t, 0)),
                  _resident((1, nv)),
                  pl.BlockSpec(st_shape, lambda b, t: (b, 0, 0, 0))],
        out_specs=[pl.BlockSpec((tm, nv), lambda b, t: (b * nt + t, 0)),
                   pl.BlockSpec(st_shape, lambda b, t: (b, 0, 0, 0))],
        out_shape=[jax.ShapeDtypeStruct((bsz * seq, nv), BF16),
                   jax.ShapeDtypeStruct((bsz,) + st_shape[1:], F32)],
        scratch_shapes=[pltpu.VMEM((RET_HEADS, RET_DK, RET_DV), F32)],
        compiler_params=_cparams("parallel", "arbitrary"),
        name="ret_core",
    )(proj, cos, sin, gn_gain.reshape(1, nv), state)


_CONV_PAD = 8


def _conv_kernel(x_ref, g_ref, w_ref, cw_ref, s0_ref, a_ref, sout_ref, z_ref):
    t = pl.program_id(1)
    tm, d = x_ref.shape
    nc = CONV_WIDTH - 1

    @pl.when(t == 0)
    def _():
        z_ref[_CONV_PAD - nc:_CONV_PAD, :] = s0_ref[0]

    h = _rms(x_ref[...], g_ref[...]).astype(BF16)
    b = _dot(h, w_ref[:, 0:d])
    z_ref[_CONV_PAD:_CONV_PAD + tm, :] = _dot(h, w_ref[:, d:2 * d]) * _dot(h, w_ref[:, 2 * d:3 * d])
    y = cw_ref[0:1, :] * z_ref[_CONV_PAD - nc:_CONV_PAD - nc + tm, :]
    for j in range(1, CONV_WIDTH):
        y = y + cw_ref[j:j + 1, :] * z_ref[_CONV_PAD - nc + j:_CONV_PAD - nc + j + tm, :]
    a_ref[...] = (b * y).astype(BF16)
    last = z_ref[_CONV_PAD + tm - nc:_CONV_PAD + tm, :]
    z_ref[_CONV_PAD - nc:_CONV_PAD, :] = last

    @pl.when(t == pl.num_programs(1) - 1)
    def _():
        sout_ref[0] = last


def _conv_core(x, g, w_in, conv_w, state, bsz, seq, tm):
    nt = seq // tm
    d = x.shape[1]
    nc = CONV_WIDTH - 1
    return pl.pallas_call(
        _conv_kernel,
        grid=(bsz, nt),
        in_specs=[pl.BlockSpec((tm, d), lambda b, t: (b * nt + t, 0)),
                  _resident((1, d)), _resident((d, 3 * d)), _resident((CONV_WIDTH, d)),
                  pl.BlockSpec((1, nc, d), lambda b, t: (b, 0, 0))],
        out_specs=[pl.BlockSpec((tm, d), lambda b, t: (b * nt + t, 0)),
                   pl.BlockSpec((1, nc, d), lambda b, t: (b, 0, 0))],
        out_shape=[jax.ShapeDtypeStruct((bsz * seq, d), BF16),
                   jax.ShapeDtypeStruct((bsz, nc, d), F32)],
        scratch_shapes=[pltpu.VMEM((_CONV_PAD + tm, d), F32)],
        compiler_params=_cparams("parallel", "arbitrary"),
        name="conv_core",
    )(x, g.reshape(1, d), w_in, conv_w, state)


_AUG = 6
LOG2E = 1.4426950408889634


def _fox_prep_kernel(x_ref, g_ref, w_ref, wf_ref, bf_ref, qg_ref, kg_ref, sel_ref, selt_ref,
                     qs_ref, kb_ref, vb_ref, vt_ref, lf_ref, kn_hbm, v_hbm, kscr, vscr, ksem, vsem):
    i = pl.program_id(0)
    n = pl.num_programs(0)
    tm, d = x_ref.shape
    slot = i % 2
    h = _rms(x_ref[...], g_ref[...]).astype(BF16)

    def head_norm(a, gain):
        ms = _dot((a * a).astype(BF16), sel_ref[...]) * (1.0 / FOX_DH)
        rs = lax.rsqrt(ms + EPS)
        hi = rs.astype(BF16)
        lo = (rs - hi.astype(F32)).astype(BF16)
        return a * (_dot(hi, selt_ref[...]) + _dot(lo, selt_ref[...])) * gain

    def head_copies(scr, hbm, sem, s, step):
        return [pltpu.make_async_copy(scr.at[s, hd], hbm.at[pl.ds(step * tm, tm), hd, :], sem.at[s])
                for hd in range(FOX_HEADS)]

    def emit_heads(val, scr, hbm, sem):
        @pl.when(i >= 2)
        def _():
            for c in head_copies(scr, hbm, sem, slot, i - 2):
                c.wait()
        for hd in range(FOX_HEADS):
            scr[slot, hd] = val[:, hd * FOX_DH:(hd + 1) * FOX_DH]
        for c in head_copies(scr, hbm, sem, slot, i):
            c.start()

    qn = head_norm(_dot(h, w_ref[:, 0:d]), qg_ref[...])
    qs_ref[...] = (qn * (FOX_DH ** -0.5 * LOG2E)).astype(BF16)
    kn = head_norm(_dot(h, w_ref[:, d:2 * d]), kg_ref[...])
    emit_heads(kn, kscr, kn_hbm, ksem)
    kb_ref[...] = kn.astype(BF16)
    v = _dot(h, w_ref[:, 2 * d:3 * d])
    emit_heads(v, vscr, v_hbm, vsem)
    vb_ref[...] = v.astype(BF16)
    vt_ref[0] = v.T.astype(BF16)
    fl = _dot(h, wf_ref[...]) + bf_ref[...]
    ls = jnp.minimum(fl, 0.0) - jnp.log(1.0 + jnp.exp(-jnp.abs(fl)))
    lane = lax.broadcasted_iota(jnp.int32, fl.shape, 1)
    lf_ref[...] = jnp.where(lane < FOX_HEADS, ls, 0.0)

    @pl.when(i == n - 1)
    def _():
        for scr, hbm, sem in ((kscr, kn_hbm, ksem), (vscr, v_hbm, vsem)):
            for c in head_copies(scr, hbm, sem, slot, i):
                c.wait()

    @pl.when((i == n - 1) & (i >= 1))
    def _():
        for scr, hbm, sem in ((kscr, kn_hbm, ksem), (vscr, v_hbm, vsem)):
            for c in head_copies(scr, hbm, sem, 1 - slot, i - 1):
                c.wait()


def _fox_prep(x, g, w_qkv, w_f, b_f, q_gain, k_gain, bsz, tm):
    m, d = x.shape
    nt = m // bsz // tm
    col = np.arange(d) // FOX_DH
    sel = (col[:, None] == np.arange(LANES)[None, :]).astype(np.float32)
    row = pl.BlockSpec((tm, d), lambda i: (i, 0))
    heads = jax.ShapeDtypeStruct((m, FOX_HEADS, FOX_DH), F32)
    return pl.pallas_call(
        _fox_prep_kernel,
        grid=(m // tm,),
        in_specs=[row, _resident((1, d)), _resident((d, 3 * d)), _resident((d, LANES)),
                  _resident((1, LANES)), _resident((1, d)), _resident((1, d)),
                  _resident((d, LANES)), _resident((LANES, d))],
        out_specs=[row, row, row,
                   pl.BlockSpec((1, d, tm), lambda i: (i // nt, 0, i % nt)),
                   pl.BlockSpec((tm, LANES), lambda i: (i, 0)),
                   pl.BlockSpec(memory_space=pl.ANY), pl.BlockSpec(memory_space=pl.ANY)],
        out_shape=[jax.ShapeDtypeStruct((m, d), BF16), jax.ShapeDtypeStruct((m, d), BF16),
                   jax.ShapeDtypeStruct((m, d), BF16),
                   jax.ShapeDtypeStruct((bsz, d, m // bsz), BF16),
                   jax.ShapeDtypeStruct((m, LANES), F32), heads, heads],
        scratch_shapes=[pltpu.VMEM((2, FOX_HEADS, tm, FOX_DH), F32),
                        pltpu.VMEM((2, FOX_HEADS, tm, FOX_DH), F32),
                        pltpu.SemaphoreType.DMA((2,)), pltpu.SemaphoreType.DMA((2,))],
        compiler_params=_cparams("arbitrary"),
        name="fox_prep",
    )(x, g.reshape(1, d), w_qkv, w_f, b_f, jnp.tile(q_gain, FOX_HEADS).reshape(1, d),
      jnp.tile(k_gain, FOX_HEADS).reshape(1, d), jnp.asarray(sel, BF16), jnp.asarray(sel.T, BF16))


def _fox_gate_kernel(lf_ref, shift_ref, pq_ref, pk_ref, cq_ref, ck_ref, qa_ref, ka_ref, carry_ref):
    t = pl.program_id(1)
    cb = lf_ref.shape[1]

    @pl.when(t == 0)
    def _():
        carry_ref[...] = jnp.zeros_like(carry_ref)

    F = _cumsum_rows(_tri(cb).astype(BF16), lf_ref[0]) + carry_ref[...]
    carry_ref[...] = F[cb - 1:cb]
    F2 = F * LOG2E
    qa = cq_ref[...]
    ka = ck_ref[...]
    for j, (pq, pk) in enumerate(zip(_split3(F2 - shift_ref[...]), _split3(F2))):
        qa = qa + _dot(pq, pq_ref[j])
        ka = ka + _dot(pk, pk_ref[j])
    qa_ref[0] = qa.astype(BF16)
    ka_ref[0] = ka.astype(BF16)


def _fox_gate(lf_all, shift):
    bsz, tk, _ = lf_all.shape
    cb = max(c for c in range(64, 769, 64) if tk % c == 0)
    place_q = np.zeros((3, LANES, LANES), np.float32)
    place_k = np.zeros((3, LANES, LANES), np.float32)
    const_q = np.zeros((1, LANES), np.float32)
    const_k = np.zeros((1, LANES), np.float32)
    for h in range(FOX_HEADS):
        for j in range(3):
            place_q[j, h, h * _AUG + j] = 1.0
            place_k[j, h, h * _AUG + 3 + j] = -1.0
            const_q[0, h * _AUG + 3 + j] = 1.0
            const_k[0, h * _AUG + j] = 1.0
    spec = pl.BlockSpec((1, cb, LANES), lambda b, t: (b, t, 0))
    return pl.pallas_call(
        _fox_gate_kernel,
        grid=(bsz, tk // cb),
        in_specs=[spec, _resident((1, LANES)), _resident(place_q.shape), _resident(place_k.shape),
                  _resident(const_q.shape), _resident(const_k.shape)],
        out_specs=[spec, spec],
        out_shape=[jax.ShapeDtypeStruct((bsz, tk, LANES), BF16)] * 2,
        scratch_shapes=[pltpu.VMEM((1, LANES), F32)],
        compiler_params=_cparams("parallel", "arbitrary"),
        name="fox_gate",
    )(lf_all, jnp.broadcast_to(shift.astype(F32), (1, LANES)),
      jnp.asarray(place_q, BF16), jnp.asarray(place_k, BF16),
      jnp.asarray(const_q), jnp.asarray(const_k))


_FOX_MAX_BOUND = 40.0


def _fox_score_bound(q_gain, k_gain, cache_lf):
    gq = jnp.max(jnp.abs(q_gain))
    gk = jnp.max(jnp.abs(k_gain))
    bound = (math.sqrt(FOX_DH) * 1.02 * LOG2E) * gq * gk
    ok = bound <= _FOX_MAX_BOUND
    if cache_lf.size:
        ok = ok & (jnp.max(cache_lf) <= 0.0)
    return jnp.where(ok, bound, 0.0), ok


def _fox_query_operands(q_ref, qa_ref, blk):
    pair = pl.program_id(1)
    lane = lax.broadcasted_iota(jnp.int32, (blk, LANES), 1)
    low = lane < FOX_DH
    q = q_ref[0]
    qa = qa_ref[0]
    qops = []
    for s in range(2):
        lo = (2 * pair + s) * _AUG
        qh = jnp.where(low if s == 0 else ~low, q, jnp.zeros_like(q))
        qah = jnp.where((lane >= lo) & (lane < lo + _AUG), qa, jnp.zeros_like(qa))
        qops.append(jnp.concatenate([qh, qah], axis=1))
    return qops, low


def _fox_key_blocks(step, carry, blk, pblk, past, nq):
    if past:
        carry = lax.fori_loop(
            0, past // pblk,
            lambda j, c: step(pl.multiple_of(j * pblk, pblk), pblk, c, False), carry)
    if nq == 1:
        return step(past, blk, carry, True)
    qi = pl.program_id(2)
    per = 2 if blk <= 512 else 1
    wide = per * blk
    nw = qi // per

    def quad(j, c):
        start = pl.multiple_of(past + j * 2 * wide, wide)
        return step(start + wide, wide, step(start, wide, c, False), False)

    carry = lax.fori_loop(0, nw // 2, quad, carry)
    carry = lax.fori_loop(
        0, nw % 2,
        lambda j, c: step(pl.multiple_of(past + (nw // 2) * 2 * wide, wide), wide, c, False), carry)
    if per == 2:
        carry = lax.fori_loop(
            0, qi % 2, lambda j, c: step(pl.multiple_of(past + (qi - 1) * blk, blk), blk, c, False),
            carry)
    return step(pl.multiple_of(past + qi * blk, blk), blk, carry, True)


def _fox_attn_online_kernel(q_ref, qa_ref, k_ref, ka_ref, v_ref, o_ref, *, blk, pblk, past, nq):
    qops, low = _fox_query_operands(q_ref, qa_ref, blk)
    causal = _tri(blk)

    def step(start, size, carry, masked):
        rows = pl.ds(start, size)
        kop = jnp.concatenate([k_ref[0, rows, :], ka_ref[0, rows, :]], axis=1)
        v = v_ref[0, rows, :]
        out = []
        for s in range(2):
            m, l, acc = carry[s]
            sc = _dot_nt(qops[s], kop)
            if masked:
                sc = jnp.where(causal, sc, -jnp.inf)
            m_new = jnp.maximum(m, jnp.max(sc, axis=-1, keepdims=True))
            alpha = jnp.exp2(m - m_new)
            p = jnp.exp2(sc - m_new)
            l = alpha * l + jnp.sum(p, axis=-1, keepdims=True)
            acc = alpha * acc + _dot(p.astype(BF16), v)
            out.append((m_new, l, acc))
        return tuple(out)

    init = tuple((jnp.full((blk, 1), -jnp.inf, F32), jnp.zeros((blk, 1), F32),
                  jnp.zeros((blk, LANES), F32)) for _ in range(2))
    (_, l0, a0), (_, l1, a1) = _fox_key_blocks(step, init, blk, pblk, past, nq)
    o_ref[0] = jnp.where(low, a0 / l0, a1 / l1).astype(BF16)


def _fox_attn_bounded_kernel(q_ref, qa_ref, k_ref, ka_ref, vt_ref, o_ref, qop_ref, acc_ref, *,
                             blk, pblk, past, nq):
    pair = pl.program_id(1)
    lane = lax.broadcasted_iota(jnp.int32, (blk, LANES), 1)
    q = q_ref[0]
    qa = qa_ref[0]
    for s in range(2):
        lo = (2 * pair + s) * _AUG
        head_lanes = (lane < FOX_DH) if s == 0 else (lane >= FOX_DH)
        qop_ref[s * blk:(s + 1) * blk, 0:LANES] = jnp.where(head_lanes, q, jnp.zeros_like(q))
        qop_ref[s * blk:(s + 1) * blk, LANES:2 * LANES] = jnp.where(
            (lane >= lo) & (lane < lo + _AUG), qa, jnp.zeros_like(qa))
    acc_ref[...] = jnp.zeros_like(acc_ref)
    r = lax.broadcasted_iota(jnp.int32, (blk, blk), 0)
    c = lax.broadcasted_iota(jnp.int32, (blk, blk), 1)
    causal = r <= c
    chan = lax.broadcasted_iota(jnp.int32, (LANES, 1), 0) < FOX_DH

    def accumulate(start, size, qop, cols, mask):
        rows = pl.ds(start, size)
        kop = jnp.concatenate([k_ref[0, rows, :], ka_ref[0, rows, :]], axis=1)
        vt = vt_ref[0, :, rows]
        ones = jnp.ones_like(vt)
        sc = _dot_nt(kop, qop)
        ncols = sc.shape[1] // 2
        s0, s1 = sc[:, :ncols], sc[:, ncols:]
        if mask is not None:
            s0 = jnp.where(mask, s0, -jnp.inf)
            s1 = jnp.where(mask, s1, -jnp.inf)
        acc_ref[0, :, cols] += _dot(jnp.where(chan, vt, ones), jnp.exp2(s0).astype(BF16))
        acc_ref[1, :, cols] += _dot(jnp.where(chan, ones, vt), jnp.exp2(s1).astype(BF16))

    half = blk // 2

    def step(start, size, carry, masked):
        if not masked:
            accumulate(start, size, qop_ref[...], slice(None), None)
        elif half % LANES:
            accumulate(start, size, qop_ref[...], slice(None), causal)
        else:
            accumulate(start, half, qop_ref[...], slice(None), causal[:half])
            late = jnp.concatenate([qop_ref[half:blk], qop_ref[blk + half:2 * blk]], axis=0)
            accumulate(start + half, half, late, slice(half, blk), causal[:half, :half])
        return carry

    _fox_key_blocks(step, 0, blk, pblk, past, nq)
    a0 = acc_ref[0]
    a1 = acc_ref[1]
    out_t = jnp.concatenate([a0[:FOX_DH] / a0[FOX_DH:], a1[FOX_DH:] / a1[:FOX_DH]], axis=0)
    o_ref[0] = out_t.T.astype(BF16)


def _fox_attn(body, transposed_v, qs, qa, kb, ka, v, bsz, seq, past, blk, pblk):
    scratch = ([pltpu.VMEM((2 * blk, 2 * LANES), BF16), pltpu.VMEM((2, LANES, blk), F32)]
               if transposed_v else [])
    tk = past + seq
    d = qs.shape[-1]
    pb = past // blk
    qspec = pl.BlockSpec((1, blk, LANES), lambda b, p, i: (b, i, p))
    kspec = pl.BlockSpec((1, tk, LANES), lambda b, p, i: (b, 0, p))
    vspec = pl.BlockSpec((1, LANES, tk), lambda b, p, i: (b, p, 0)) if transposed_v else kspec
    return pl.pallas_call(
        functools.partial(body, blk=blk, pblk=pblk, past=past, nq=seq // blk),
        grid=(bsz, d // LANES, seq // blk),
        in_specs=[qspec,
                  pl.BlockSpec((1, blk, LANES), lambda b, p, i: (b, pb + i, 0)),
                  kspec,
                  pl.BlockSpec((1, tk, LANES), lambda b, p, i: (b, 0, 0)),
                  vspec],
        out_specs=qspec,
        out_shape=jax.ShapeDtypeStruct((bsz, seq, d), BF16),
        scratch_shapes=scratch,
        compiler_params=_cparams("parallel", "parallel", "arbitrary"),
        name=body.__name__.strip("_").replace("_kernel", ""),
    )(qs, qa, kb, ka, v)


def _trunk(x, pos0, hg_s, ret_s, conv_s, fox_k, fox_v, fox_lf, W):
    bsz, seq, d = x.shape
    m = bsz * seq
    past = fox_k.shape[1]
    tm = min(seq, 256)
    tm_proj = min(seq, 512)
    tm_ffn = min(m, 512)
    x = x.reshape(m, d)

    def ffn(x, a, w_out, layer):
        return _outproj_ffn(x, a, w_out, W["norm_ffn"][layer], W["ffn_w_g"][layer],
                            W["ffn_w_u"][layer], W["ffn_w_d"][layer], tm_ffn, FFN_BOUNDS)

    proj = _norm_proj(x, W["norm_mix"][0], W["hg_w_in"], tm_proj, 512)
    a, hg_s = _hgrn_core(proj, W["hg_lower_bounds"], W["hg_out_gain"], hg_s, 0, bsz, seq, tm)
    x = ffn(x, a, W["hg_w_out"], 0)

    proj = _norm_proj(x, W["norm_mix"][1], W["ret_w_in"], tm_proj, 512)
    inv = jnp.power(ROPE_BASE, -jnp.arange(0, RET_DK, 2, dtype=F32) / RET_DK)
    cos, sin = _rope_table(inv, pos0, seq, tm)
    a, ret_s = _ret_core(proj, cos, sin, W["ret_gn_gain"], ret_s, bsz, seq, tm, min(seq, 256))
    x = ffn(x, a, W["ret_w_out"], 1)

    a, conv_s = _conv_core(x, W["norm_mix"][2], W["conv_w_in"], W["conv_w"], conv_s, bsz, seq,
                            tm_proj)
    x = ffn(x, a, W["conv_w_out"], 2)

    qs, kb, vb, vt, lf, kn, v = _fox_prep(x, W["norm_mix"][3], W["fox_w_qkv"], W["fox_w_f"],
                                          W["fox_b_f"], W["fox_q_gain"], W["fox_k_gain"], bsz,
                                          tm_proj)
    blk = min(seq, 1024)
    lf3 = lf.reshape(bsz, seq, LANES)
    kb3 = kb.reshape(bsz, seq, d)
    vb3 = vb.reshape(bsz, seq, d)
    if past:
        lf3 = jnp.concatenate([jnp.pad(fox_lf, ((0, 0), (0, 0), (0, LANES - FOX_HEADS))), lf3], axis=1)
        kb3 = jnp.concatenate([fox_k.reshape(bsz, past, d).astype(BF16), kb3], axis=1)
        past_v = fox_v.reshape(bsz, past, d).astype(BF16)
        vb3 = jnp.concatenate([past_v, vb3], axis=1)
        vt = jnp.concatenate([past_v.transpose(0, 2, 1), vt], axis=2)
    shift, bounded = _fox_score_bound(W["fox_q_gain"], W["fox_k_gain"], fox_lf)
    qa, ka = _fox_gate(lf3, shift)
    attn = functools.partial(_fox_attn, bsz=bsz, seq=seq, past=past, blk=blk,
                             pblk=min(past, 512) if past else blk)
    q3 = qs.reshape(bsz, seq, d)
    a = lax.cond(bounded,
                 lambda: attn(_fox_attn_bounded_kernel, True, q3, qa, kb3, ka, vt),
                 lambda: attn(_fox_attn_online_kernel, False, q3, qa, kb3, ka, vb3))
    x = ffn(x, a.reshape(m, d), W["fox_w_out"], 3)

    return (x.reshape(bsz, seq, d), hg_s, ret_s, conv_s,
            kn.reshape(bsz, seq, FOX_HEADS, FOX_DH), v.reshape(bsz, seq, FOX_HEADS, FOX_DH),
            lf[:, :FOX_HEADS].reshape(bsz, seq, FOX_HEADS))


def kernel(x_prompt, x_sample, state_hgrn, state_ret, state_conv, cache_fox_k, cache_fox_v, cache_fox_logf, norm_mix, norm_ffn, hg_w_in, hg_lower_bounds, hg_out_gain, hg_w_out, ret_w_in, ret_gn_gain, ret_w_out, conv_w_in, conv_w, conv_w_out, fox_w_in, fox_b_f, fox_q_gain, fox_k_gain, fox_w_out, ffn_w_gu, ffn_w_down):
    d = D_MODEL
    bf = lambda w: w.astype(BF16)
    W = dict(
        norm_mix=norm_mix, norm_ffn=norm_ffn,
        hg_w_in=bf(hg_w_in), hg_lower_bounds=hg_lower_bounds, hg_out_gain=hg_out_gain,
        hg_w_out=bf(hg_w_out),
        ret_w_in=bf(ret_w_in), ret_gn_gain=ret_gn_gain, ret_w_out=bf(ret_w_out),
        conv_w_in=bf(conv_w_in), conv_w=conv_w, conv_w_out=bf(conv_w_out),
        fox_w_qkv=bf(fox_w_in[:, :3 * d]),
        fox_w_f=bf(jnp.pad(fox_w_in[:, 3 * d:], ((0, 0), (0, LANES - FOX_HEADS)))),
        fox_b_f=jnp.pad(fox_b_f, (0, LANES - FOX_HEADS)).reshape(1, LANES),
        fox_q_gain=fox_q_gain, fox_k_gain=fox_k_gain, fox_w_out=bf(fox_w_out),
        ffn_w_g=bf(ffn_w_gu[:, :, :D_FF]), ffn_w_u=bf(ffn_w_gu[:, :, D_FF:]), ffn_w_d=bf(ffn_w_down),
    )
    bsz = x_prompt.shape[0]
    dt = x_prompt.dtype
    (y_p, hg_p, ret_p, conv_p, fk_p, fv_p, flf_p) = _trunk(
        x_prompt, 0,
        jnp.zeros((bsz, HG_HEADS, HG_DK, HG_DV), F32),
        jnp.zeros((bsz, RET_HEADS, RET_DK, RET_DV), F32),
        jnp.zeros((bsz, CONV_WIDTH - 1, d), dt),
        jnp.zeros((bsz, 0, FOX_HEADS, FOX_DH), dt),
        jnp.zeros((bsz, 0, FOX_HEADS, FOX_DH), dt),
        jnp.zeros((bsz, 0, FOX_HEADS), F32), W)
    past = cache_fox_k.shape[1]
    (y_s, hg_s, ret_s, conv_s, fk_s, fv_s, flf_s) = _trunk(
        x_sample, past, state_hgrn, state_ret, state_conv,
        cache_fox_k, cache_fox_v, cache_fox_logf, W)
    return (y_p, y_s, hg_p, hg_s, ret_p, ret_s, conv_p, conv_s,
            fk_p, fv_p, flf_p, fk_s, fv_s, flf_s)
```

```python
import functools
import math

import numpy as np
import jax
import jax.numpy as jnp
from jax import lax
from jax.experimental import pallas as pl
from jax.experimental.pallas import tpu as pltpu

F32 = jnp.float32
BF16 = jnp.bfloat16

D_MODEL = 1024
EPS = 1e-6
HG_HEADS, HG_DK, HG_DV = 8, 128, 128
HG_CHUNK = 64
RET_HEADS, RET_DK, RET_DV = 4, 256, 512
ROPE_BASE = 10000.0
CONV_WIDTH = 3
FOX_HEADS, FOX_DH = 16, 64
D_FF = 2816
MXU_DIM = 256
FFN_BOUNDS = (0, (D_FF // MXU_DIM + 1) // 2 * MXU_DIM, D_FF)
LANES = 128

VMEM_LIMIT = 56 * 1024 * 1024


def _cparams(*sem):
    return pltpu.CompilerParams(dimension_semantics=sem, vmem_limit_bytes=VMEM_LIMIT)


def _resident(shape):
    nd = len(shape)
    return pl.BlockSpec(shape, lambda *_: (0,) * nd, pipeline_mode=pl.Buffered(1))


def _dot(a, b):
    return jnp.dot(a, b, preferred_element_type=F32)


def _dot_nt(a, b):
    return lax.dot_general(a, b, (((1,), (1,)), ((), ())), preferred_element_type=F32)


def _dot_tn(a, b):
    return lax.dot_general(a, b, (((0,), (0,)), ((), ())), preferred_element_type=F32)


def _sigmoid(x):
    return 0.5 * jnp.tanh(0.5 * x) + 0.5


def _rms(x, g):
    return x * lax.rsqrt(jnp.mean(x * x, axis=-1, keepdims=True) + EPS) * g


def _split3(x):
    hi = x.astype(BF16)
    r = x - hi.astype(F32)
    mid = r.astype(BF16)
    lo = (r - mid.astype(F32)).astype(BF16)
    return hi, mid, lo


def _tri(n):
    r = lax.broadcasted_iota(jnp.int32, (n, n), 0)
    c = lax.broadcasted_iota(jnp.int32, (n, n), 1)
    return r >= c


def _cumsum_rows(tri_bf16, x):
    hi, mid, lo = _split3(x)
    return _dot(tri_bf16, hi) + _dot(tri_bf16, mid) + _dot(tri_bf16, lo)


def _norm_proj_kernel(x_ref, g_ref, w_ref, o_ref, *, tn):
    h = _rms(x_ref[...], g_ref[...]).astype(BF16)
    for c in range(o_ref.shape[1] // tn):
        o_ref[:, c * tn:(c + 1) * tn] = _dot(h, w_ref[:, c * tn:(c + 1) * tn])


def _norm_proj(x, g, w, tm, tn):
    m, d = x.shape
    n = w.shape[1]
    return pl.pallas_call(
        functools.partial(_norm_proj_kernel, tn=tn),
        grid=(m // tm,),
        in_specs=[pl.BlockSpec((tm, d), lambda i: (i, 0)), _resident((1, d)), _resident((d, n))],
        out_specs=pl.BlockSpec((tm, n), lambda i: (i, 0)),
        out_shape=jax.ShapeDtypeStruct((m, n), F32),
        compiler_params=_cparams("parallel"),
        name="norm_proj",
    )(x, g.reshape(1, d), w)


def _ffn_kernel(x_ref, a_ref, wo_ref, g_ref, wg_ref, wu_ref, wd_ref, y_ref, *, bounds):
    tm = x_ref.shape[0]
    groups = [slice(0, tm // 2), slice(tm // 2, tm)] if tm % 32 == 0 else [slice(0, tm)]
    x1 = [x_ref[r, :] + _dot(a_ref[r, :], wo_ref[...]) for r in groups]
    h = [_rms(x, g_ref[...]).astype(BF16) for x in x1]
    acc = list(x1)
    for lo, hi in zip(bounds[:-1], bounds[1:]):
        gate = [_dot(hh, wg_ref[:, lo:hi]) for hh in h]
        up = [_dot(hh, wu_ref[:, lo:hi]) for hh in h]
        act = [(g * _sigmoid(g) * u).astype(BF16) for g, u in zip(gate, up)]
        acc = [a + _dot(t, wd_ref[lo:hi, :]) for a, t in zip(acc, act)]
    for r, a in zip(groups, acc):
        y_ref[r, :] = a


def _outproj_ffn(x, a, w_out, g, w_g, w_u, w_d, tm, bounds):
    m, d = x.shape
    ka = a.shape[1]
    return pl.pallas_call(
        functools.partial(_ffn_kernel, bounds=bounds),
        grid=(m // tm,),
        in_specs=[pl.BlockSpec((tm, d), lambda i: (i, 0)),
                  pl.BlockSpec((tm, ka), lambda i: (i, 0)),
                  _resident((ka, d)), _resident((1, d)),
                  _resident((d, D_FF)), _resident((d, D_FF)), _resident((D_FF, d))],
        out_specs=pl.BlockSpec((tm, d), lambda i: (i, 0)),
        out_shape=jax.ShapeDtypeStruct((m, d), F32),
        compiler_params=_cparams("parallel"),
        name="outproj_ffn",
    )(x, a, w_out, g.reshape(1, d), w_g, w_u, w_d)


def _hgrn_kernel(p_ref, lbw_ref, gain_ref, s0_ref, o_ref, sout_ref, st_ref, *, layer, nchunk):
    t = pl.program_id(1)
    L = HG_CHUNK
    nk = HG_HEADS * HG_DK

    @pl.when(t == 0)
    def _():
        for h in range(HG_HEADS):
            st_ref[h] = s0_ref[0, h].T

    lbw = lbw_ref[...]
    e = jnp.exp(lbw - jnp.max(lbw, axis=0, keepdims=True))
    lb = jnp.sum(e[:layer + 1], axis=0, keepdims=True) / jnp.sum(e, axis=0, keepdims=True)
    gain = gain_ref[...]
    tm = p_ref.shape[0]

    r = lax.broadcasted_iota(jnp.int32, (tm, tm), 0)
    c = lax.broadcasted_iota(jnp.int32, (tm, tm), 1)
    shift = L.bit_length() - 1
    tri = (r >= c) & ((r >> shift) == (c >> shift))
    q = p_ref[:, 0:nk]
    fg = lb + (1.0 - lb) * _sigmoid(p_ref[:, nk:2 * nk])
    k = 1.0 - fg
    G = _cumsum_rows(tri.astype(BF16), jnp.log(fg))

    def per_chunk_row(offset):
        return jnp.concatenate(
            [jnp.broadcast_to(G[i * L + offset:i * L + offset + 1], (L, nk)) for i in range(nchunk)],
            axis=0)

    g_mid = per_chunk_row(L // 2)
    g_last = per_chunk_row(L - 1)
    qa = (q * jnp.exp(G - g_mid)).astype(BF16)
    kb = (k * jnp.exp(g_mid - G)).astype(BF16)
    qs = (q * jnp.exp(G)).astype(BF16)
    kl = (k * jnp.exp(g_last - G)).astype(BF16)
    heads = range(HG_HEADS)
    sls = [slice(h * HG_DK, (h + 1) * HG_DK) for h in heads]
    vs = [p_ref[:, 2 * nk + h * HG_DV:2 * nk + (h + 1) * HG_DV].astype(BF16) for h in heads]
    scs = [jnp.where(tri, _dot_nt(qa[:, sl], kb[:, sl]), 0.0).astype(BF16) for sl in sls]
    o_intra = [_dot(sc, v) for sc, v in zip(scs, vs)]
    us = [[_dot_tn(v[i * L:(i + 1) * L], kl[i * L:(i + 1) * L, sl]) for i in range(nchunk)]
          for v, sl in zip(vs, sls)]
    sts = [st_ref[h] for h in heads]
    outs = [[] for _ in heads]
    for i in range(nchunk):
        rows = slice(i * L, (i + 1) * L)
        for h in heads:
            outs[h].append(o_intra[h][rows] + _dot_nt(qs[rows, sls[h]], sts[h].astype(BF16)))
            sts[h] = sts[h] * jnp.exp(G[(i + 1) * L - 1:(i + 1) * L, sls[h]]) + us[h][i]
    for h in heads:
        st_ref[h] = sts[h]
        gate = p_ref[:, 3 * nk + h * HG_DV:3 * nk + (h + 1) * HG_DV]
        o_ref[:, h * HG_DV:(h + 1) * HG_DV] = (
            _rms(jnp.concatenate(outs[h], axis=0), gain) * (gate * _sigmoid(gate))).astype(BF16)

    @pl.when(t == pl.num_programs(1) - 1)
    def _():
        for h in range(HG_HEADS):
            sout_ref[0, h] = st_ref[h].T


def _hgrn_core(proj, lower_bounds, out_gain, state, layer, bsz, seq, tm):
    nt = seq // tm
    w = proj.shape[1]
    st_shape = (1, HG_HEADS, HG_DK, HG_DV)
    return pl.pallas_call(
        functools.partial(_hgrn_kernel, layer=layer, nchunk=tm // HG_CHUNK),
        grid=(bsz, nt),
        in_specs=[pl.BlockSpec((tm, w), lambda b, t: (b * nt + t, 0)),
                  _resident(lower_bounds.shape), _resident((1, HG_DV)),
                  pl.BlockSpec(st_shape, lambda b, t: (b, 0, 0, 0))],
        out_specs=[pl.BlockSpec((tm, HG_HEADS * HG_DV), lambda b, t: (b * nt + t, 0)),
                   pl.BlockSpec(st_shape, lambda b, t: (b, 0, 0, 0))],
        out_shape=[jax.ShapeDtypeStruct((bsz * seq, HG_HEADS * HG_DV), BF16),
                   jax.ShapeDtypeStruct((bsz,) + st_shape[1:], F32)],
        scratch_shapes=[pltpu.VMEM((HG_HEADS, HG_DV, HG_DK), F32)],
        compiler_params=_cparams("parallel", "arbitrary"),
        name="hgrn_core",
    )(proj, lower_bounds, out_gain.reshape(1, HG_DV), state)


def _rope_table_kernel(inv_ref, cos_ref, sin_ref, *, pos0):
    tb = cos_ref.shape[0]
    pos = (pos0 + pl.program_id(0) * tb
           + lax.broadcasted_iota(jnp.int32, cos_ref.shape, 0)).astype(F32)
    ang = pos * inv_ref[...]
    cos_ref[...] = jnp.cos(ang)
    sin_ref[...] = jnp.sin(ang)


def _rope_table(inv, pos0, seq, tb):
    half = inv.shape[0]
    spec = pl.BlockSpec((tb, half), lambda i: (i, 0))
    return pl.pallas_call(
        functools.partial(_rope_table_kernel, pos0=pos0),
        grid=(seq // tb,),
        in_specs=[_resident((1, half))],
        out_specs=[spec, spec],
        out_shape=[jax.ShapeDtypeStruct((seq, half), F32)] * 2,
        compiler_params=_cparams("parallel"),
        name="rope_table",
    )(inv.reshape(1, half))


def _ret_proj_kernel(x_ref, g_ref, w_ref, cos_ref, sin_ref, qr_ref, kb_ref, kd_ref, v_ref, sg_ref,
                     *, L):
    tm = x_ref.shape[0]
    nq = RET_HEADS * RET_DK
    nv = RET_HEADS * RET_DV
    half = RET_DK // 2
    hx = _rms(x_ref[...], g_ref[...]).astype(BF16)
    cos = cos_ref[...]
    sin = sin_ref[...]
    ti = (lax.broadcasted_iota(jnp.int32, (tm, 1), 0) % L).astype(F32)
    scale = RET_DK ** -0.5

    def rope(a):
        a1, a2 = a[:, :half], a[:, half:]
        return jnp.concatenate([a1 * cos - a2 * sin, a1 * sin + a2 * cos], axis=1)

    for h in range(RET_HEADS):
        lg = math.log(1.0 - 2.0 ** (-5.0 - h))
        cq = slice(h * RET_DK, (h + 1) * RET_DK)
        cv = slice(h * RET_DV, (h + 1) * RET_DV)
        qr_ref[:, cq] = rope(_dot(hx, w_ref[:, cq])).astype(BF16)
        kr = rope(_dot(hx, w_ref[:, nq + h * RET_DK:nq + (h + 1) * RET_DK])) * scale
        kb_ref[:, cq] = kr.astype(BF16)
        kd_ref[:, cq] = (kr * jnp.exp(lg * (L - 1.0 - ti))).astype(BF16)
        v_ref[:, cv] = _dot(hx, w_ref[:, 2 * nq + h * RET_DV:2 * nq + (h + 1) * RET_DV]).astype(BF16)
        gate = _dot(hx, w_ref[:, 2 * nq + nv + h * RET_DV:2 * nq + nv + (h + 1) * RET_DV])
        sg_ref[:, cv] = gate * _sigmoid(gate)


def _ret_proj(x, g, w, cos, sin, seq, tm, chunk):
    m, d = x.shape
    nq = RET_HEADS * RET_DK
    nv = RET_HEADS * RET_DV
    half = RET_DK // 2
    ntp = seq // tm
    row = lambda n: pl.BlockSpec((tm, n), lambda i: (i, 0))
    pos = pl.BlockSpec((tm, half), lambda i: (i % ntp, 0))
    return pl.pallas_call(
        functools.partial(_ret_proj_kernel, L=chunk),
        grid=(m // tm,),
        in_specs=[row(d), _resident((1, d)), _resident(w.shape), pos, pos],
        out_specs=[row(nq), row(nq), row(nq), row(nv), row(nv)],
        out_shape=[jax.ShapeDtypeStruct((m, nq), BF16)] * 3
        + [jax.ShapeDtypeStruct((m, nv), BF16), jax.ShapeDtypeStruct((m, nv), F32)],
        compiler_params=_cparams("parallel"),
        name="ret_proj",
    )(x, g.reshape(1, d), w, cos, sin)


def _ret_kernel(qr_ref, kb_ref, kd_ref, v_ref, sg_ref, decay_ref, gn_ref, s0_ref, o_ref, sout_ref,
                st_ref, *, L, nchunk):
    t = pl.program_id(1)

    @pl.when(t == 0)
    def _():
        st_ref[...] = s0_ref[0]

    ti = lax.broadcasted_iota(jnp.int32, (L, 1), 0).astype(F32)

    def chunk(c, carry):
        rows = pl.ds(pl.multiple_of(c * L, L), L)
        heads = range(RET_HEADS)
        lgs = [math.log(1.0 - 2.0 ** (-5.0 - h)) for h in heads]
        cqs = [slice(h * RET_DK, (h + 1) * RET_DK) for h in heads]
        cvs = [slice(h * RET_DV, (h + 1) * RET_DV) for h in heads]
        qrs = [qr_ref[rows, cq] for cq in cqs]
        vs = [v_ref[rows, cv] for cv in cvs]
        scs = [(_dot_nt(qrs[h], kb_ref[rows, cqs[h]]) * decay_ref[h]).astype(BF16) for h in heads]
        sts = [st_ref[h] for h in heads]
        os = [_dot(scs[h], vs[h]) + _dot(qrs[h], sts[h].astype(BF16)) * jnp.exp(lgs[h] * (ti + 1.0))
              for h in heads]
        for h in heads:
            st_ref[h] = math.exp(lgs[h] * L) * sts[h] + _dot_tn(kd_ref[rows, cqs[h]], vs[h])
        for h in heads:
            o = os[h]
            mu = jnp.mean(o, axis=-1, keepdims=True)
            d = o - mu
            var = jnp.mean(d * d, axis=-1, keepdims=True)
            on = d * lax.rsqrt(var + EPS) * gn_ref[:, cvs[h]]
            o_ref[rows, cvs[h]] = (on * sg_ref[rows, cvs[h]]).astype(BF16)
        return carry

    lax.fori_loop(0, nchunk, chunk, 0)

    @pl.when(t == pl.num_programs(1) - 1)
    def _():
        sout_ref[0] = st_ref[...]


def _ret_core(qr, kb, kd, v, sg, gn_gain, state, bsz, seq, tm, chunk):
    lg = jnp.log(1.0 - jnp.power(2.0, -5.0 - jnp.arange(RET_HEADS, dtype=F32)))[:, None, None]
    idx = jnp.arange(chunk, dtype=F32)
    diff = idx[:, None] - idx[None, :]
    decay = jnp.where(diff >= 0, jnp.exp(lg * jnp.maximum(diff, 0.0)), 0.0)
    nt = seq // tm
    nq = RET_HEADS * RET_DK
    nv = RET_HEADS * RET_DV
    st_shape = (1, RET_HEADS, RET_DK, RET_DV)
    row = lambda n: pl.BlockSpec((tm, n), lambda b, t: (b * nt + t, 0))
    return pl.pallas_call(
        functools.partial(_ret_kernel, L=chunk, nchunk=tm // chunk),
        grid=(bsz, nt),
        in_specs=[row(nq), row(nq), row(nq), row(nv), row(nv), _resident(decay.shape),
                  _resident((1, nv)), pl.BlockSpec(st_shape, lambda b, t: (b, 0, 0, 0))],
        out_specs=[row(nv), pl.BlockSpec(st_shape, lambda b, t: (b, 0, 0, 0))],
        out_shape=[jax.ShapeDtypeStruct((bsz * seq, nv), BF16),
                   jax.ShapeDtypeStruct((bsz,) + st_shape[1:], F32)],
        scratch_shapes=[pltpu.VMEM((RET_HEADS, RET_DK, RET_DV), F32)],
        compiler_params=_cparams("parallel", "arbitrary"),
        name="ret_core",
    )(qr, kb, kd, v, sg, decay, gn_gain.reshape(1, nv), state)


_CONV_PAD = 8


def _conv_kernel(x_ref, g_ref, w_ref, cw_ref, s0_ref, a_ref, sout_ref, z_ref):
    t = pl.program_id(1)
    tm, d = x_ref.shape
    nc = CONV_WIDTH - 1

    @pl.when(t == 0)
    def _():
        z_ref[_CONV_PAD - nc:_CONV_PAD, :] = s0_ref[0]

    h = _rms(x_ref[...], g_ref[...]).astype(BF16)
    b = _dot(h, w_ref[:, 0:d])
    z_ref[_CONV_PAD:_CONV_PAD + tm, :] = _dot(h, w_ref[:, d:2 * d]) * _dot(h, w_ref[:, 2 * d:3 * d])
    y = cw_ref[0:1, :] * z_ref[_CONV_PAD - nc:_CONV_PAD - nc + tm, :]
    for j in range(1, CONV_WIDTH):
        y = y + cw_ref[j:j + 1, :] * z_ref[_CONV_PAD - nc + j:_CONV_PAD - nc + j + tm, :]
    a_ref[...] = (b * y).astype(BF16)
    last = z_ref[_CONV_PAD + tm - nc:_CONV_PAD + tm, :]
    z_ref[_CONV_PAD - nc:_CONV_PAD, :] = last

    @pl.when(t == pl.num_programs(1) - 1)
    def _():
        sout_ref[0] = last


def _conv_core(x, g, w_in, conv_w, state, bsz, seq, tm):
    nt = seq // tm
    d = x.shape[1]
    nc = CONV_WIDTH - 1
    return pl.pallas_call(
        _conv_kernel,
        grid=(bsz, nt),
        in_specs=[pl.BlockSpec((tm, d), lambda b, t: (b * nt + t, 0)),
                  _resident((1, d)), _resident((d, 3 * d)), _resident((CONV_WIDTH, d)),
                  pl.BlockSpec((1, nc, d), lambda b, t: (b, 0, 0))],
        out_specs=[pl.BlockSpec((tm, d), lambda b, t: (b * nt + t, 0)),
                   pl.BlockSpec((1, nc, d), lambda b, t: (b, 0, 0))],
        out_shape=[jax.ShapeDtypeStruct((bsz * seq, d), BF16),
                   jax.ShapeDtypeStruct((bsz, nc, d), F32)],
        scratch_shapes=[pltpu.VMEM((_CONV_PAD + tm, d), F32)],
        compiler_params=_cparams("parallel", "arbitrary"),
        name="conv_core",
    )(x, g.reshape(1, d), w_in, conv_w, state)


_AUG = 6
LOG2E = 1.4426950408889634


def _fox_prep_kernel(x_ref, g_ref, w_ref, wf_ref, bf_ref, qg_ref, kg_ref, sel_ref, selt_ref,
                     qs_ref, kb_ref, vb_ref, vt_ref, lf_ref, kn_hbm, v_hbm, kscr, vscr, ksem, vsem):
    i = pl.program_id(0)
    n = pl.num_programs(0)
    tm, d = x_ref.shape
    slot = i % 2
    h = _rms(x_ref[...], g_ref[...]).astype(BF16)

    def head_norm(a, gain):
        ms = _dot((a * a).astype(BF16), sel_ref[...]) * (1.0 / FOX_DH)
        rs = lax.rsqrt(ms + EPS)
        hi = rs.astype(BF16)
        lo = (rs - hi.astype(F32)).astype(BF16)
        return a * (_dot(hi, selt_ref[...]) + _dot(lo, selt_ref[...])) * gain

    def head_copies(scr, hbm, sem, s, step):
        return [pltpu.make_async_copy(scr.at[s, hd], hbm.at[pl.ds(step * tm, tm), hd, :], sem.at[s])
                for hd in range(FOX_HEADS)]

    def emit_heads(val, scr, hbm, sem):
        @pl.when(i >= 2)
        def _():
            for c in head_copies(scr, hbm, sem, slot, i - 2):
                c.wait()
        for hd in range(FOX_HEADS):
            scr[slot, hd] = val[:, hd * FOX_DH:(hd + 1) * FOX_DH]
        for c in head_copies(scr, hbm, sem, slot, i):
            c.start()

    qn = head_norm(_dot(h, w_ref[:, 0:d]), qg_ref[...])
    qs_ref[...] = (qn * (FOX_DH ** -0.5 * LOG2E)).astype(BF16)
    kn = head_norm(_dot(h, w_ref[:, d:2 * d]), kg_ref[...])
    emit_heads(kn, kscr, kn_hbm, ksem)
    kb_ref[...] = kn.astype(BF16)
    v = _dot(h, w_ref[:, 2 * d:3 * d])
    emit_heads(v, vscr, v_hbm, vsem)
    vb_ref[...] = v.astype(BF16)
    vt_ref[0] = v.T.astype(BF16)
    fl = _dot(h, wf_ref[...]) + bf_ref[...]
    ls = jnp.minimum(fl, 0.0) - jnp.log(1.0 + jnp.exp(-jnp.abs(fl)))
    lane = lax.broadcasted_iota(jnp.int32, fl.shape, 1)
    lf_ref[...] = jnp.where(lane < FOX_HEADS, ls, 0.0)

    @pl.when(i == n - 1)
    def _():
        for scr, hbm, sem in ((kscr, kn_hbm, ksem), (vscr, v_hbm, vsem)):
            for c in head_copies(scr, hbm, sem, slot, i):
                c.wait()

    @pl.when((i == n - 1) & (i >= 1))
    def _():
        for scr, hbm, sem in ((kscr, kn_hbm, ksem), (vscr, v_hbm, vsem)):
            for c in head_copies(scr, hbm, sem, 1 - slot, i - 1):
                c.wait()


def _fox_prep(x, g, w_qkv, w_f, b_f, q_gain, k_gain, bsz, tm):
    m, d = x.shape
    nt = m // bsz // tm
    col = np.arange(d) // FOX_DH
    sel = (col[:, None] == np.arange(LANES)[None, :]).astype(np.float32)
    row = pl.BlockSpec((tm, d), lambda i: (i, 0))
    heads = jax.ShapeDtypeStruct((m, FOX_HEADS, FOX_DH), F32)
    return pl.pallas_call(
        _fox_prep_kernel,
        grid=(m // tm,),
        in_specs=[row, _resident((1, d)), _resident((d, 3 * d)), _resident((d, LANES)),
                  _resident((1, LANES)), _resident((1, d)), _resident((1, d)),
                  _resident((d, LANES)), _resident((LANES, d))],
        out_specs=[row, row, row,
                   pl.BlockSpec((1, d, tm), lambda i: (i // nt, 0, i % nt)),
                   pl.BlockSpec((tm, LANES), lambda i: (i, 0)),
                   pl.BlockSpec(memory_space=pl.ANY), pl.BlockSpec(memory_space=pl.ANY)],
        out_shape=[jax.ShapeDtypeStruct((m, d), BF16), jax.ShapeDtypeStruct((m, d), BF16),
                   jax.ShapeDtypeStruct((m, d), BF16),
                   jax.ShapeDtypeStruct((bsz, d, m // bsz), BF16),
                   jax.ShapeDtypeStruct((m, LANES), F32), heads, heads],
        scratch_shapes=[pltpu.VMEM((2, FOX_HEADS, tm, FOX_DH), F32),
                        pltpu.VMEM((2, FOX_HEADS, tm, FOX_DH), F32),
                        pltpu.SemaphoreType.DMA((2,)), pltpu.SemaphoreType.DMA((2,))],
        compiler_params=_cparams("arbitrary"),
        name="fox_prep",
    )(x, g.reshape(1, d), w_qkv, w_f, b_f, jnp.tile(q_gain, FOX_HEADS).reshape(1, d),
      jnp.tile(k_gain, FOX_HEADS).reshape(1, d), jnp.asarray(sel, BF16), jnp.asarray(sel.T, BF16))


def _fox_gate_kernel(lf_ref, shift_ref, pq_ref, pk_ref, cq_ref, ck_ref, qa_ref, ka_ref, carry_ref):
    t = pl.program_id(1)
    cb = lf_ref.shape[1]

    @pl.when(t == 0)
    def _():
        carry_ref[...] = jnp.zeros_like(carry_ref)

    F = _cumsum_rows(_tri(cb).astype(BF16), lf_ref[0]) + carry_ref[...]
    carry_ref[...] = F[cb - 1:cb]
    F2 = F * LOG2E
    qa = cq_ref[...]
    ka = ck_ref[...]
    for j, (pq, pk) in enumerate(zip(_split3(F2 - shift_ref[...]), _split3(F2))):
        qa = qa + _dot(pq, pq_ref[j])
        ka = ka + _dot(pk, pk_ref[j])
    qa_ref[0] = qa.astype(BF16)
    ka_ref[0] = ka.astype(BF16)


def _fox_gate(lf_all, shift):
    bsz, tk, _ = lf_all.shape
    cb = max(c for c in range(64, 769, 64) if tk % c == 0)
    place_q = np.zeros((3, LANES, LANES), np.float32)
    place_k = np.zeros((3, LANES, LANES), np.float32)
    const_q = np.zeros((1, LANES), np.float32)
    const_k = np.zeros((1, LANES), np.float32)
    for h in range(FOX_HEADS):
        for j in range(3):
            place_q[j, h, h * _AUG + j] = 1.0
            place_k[j, h, h * _AUG + 3 + j] = -1.0
            const_q[0, h * _AUG + 3 + j] = 1.0
            const_k[0, h * _AUG + j] = 1.0
    spec = pl.BlockSpec((1, cb, LANES), lambda b, t: (b, t, 0))
    return pl.pallas_call(
        _fox_gate_kernel,
        grid=(bsz, tk // cb),
        in_specs=[spec, _resident((1, LANES)), _resident(place_q.shape), _resident(place_k.shape),
                  _resident(const_q.shape), _resident(const_k.shape)],
        out_specs=[spec, spec],
        out_shape=[jax.ShapeDtypeStruct((bsz, tk, LANES), BF16)] * 2,
        scratch_shapes=[pltpu.VMEM((1, LANES), F32)],
        compiler_params=_cparams("parallel", "arbitrary"),
        name="fox_gate",
    )(lf_all, jnp.broadcast_to(shift.astype(F32), (1, LANES)),
      jnp.asarray(place_q, BF16), jnp.asarray(place_k, BF16),
      jnp.asarray(const_q), jnp.asarray(const_k))


_FOX_MAX_BOUND = 40.0


def _fox_score_bound(q_gain, k_gain, cache_lf):
    gq = jnp.max(jnp.abs(q_gain))
    gk = jnp.max(jnp.abs(k_gain))
    bound = (math.sqrt(FOX_DH) * 1.02 * LOG2E) * gq * gk
    ok = bound <= _FOX_MAX_BOUND
    if cache_lf.size:
        ok = ok & (jnp.max(cache_lf) <= 0.0)
    return jnp.where(ok, bound, 0.0), ok


def _fox_query_operands(q_ref, qa_ref, blk):
    pair = pl.program_id(1)
    lane = lax.broadcasted_iota(jnp.int32, (blk, LANES), 1)
    low = lane < FOX_DH
    q = q_ref[0]
    qa = qa_ref[0]
    qops = []
    for s in range(2):
        lo = (2 * pair + s) * _AUG
        qh = jnp.where(low if s == 0 else ~low, q, jnp.zeros_like(q))
        qah = jnp.where((lane >= lo) & (lane < lo + _AUG), qa, jnp.zeros_like(qa))
        qops.append(jnp.concatenate([qh, qah], axis=1))
    return qops, low


def _fox_key_blocks(step, carry, blk, pblk, past, nq):
    if past:
        carry = lax.fori_loop(
            0, past // pblk,
            lambda j, c: step(pl.multiple_of(j * pblk, pblk), pblk, c, False), carry)
    if nq == 1:
        return step(past, blk, carry, True)
    qi = pl.program_id(2)
    per = 2 if blk <= 512 else 1
    wide = per * blk
    nw = qi // per

    def quad(j, c):
        start = pl.multiple_of(past + j * 2 * wide, wide)
        return step(start + wide, wide, step(start, wide, c, False), False)

    carry = lax.fori_loop(0, nw // 2, quad, carry)
    carry = lax.fori_loop(
        0, nw % 2,
        lambda j, c: step(pl.multiple_of(past + (nw // 2) * 2 * wide, wide), wide, c, False), carry)
    if per == 2:
        carry = lax.fori_loop(
            0, qi % 2, lambda j, c: step(pl.multiple_of(past + (qi - 1) * blk, blk), blk, c, False),
            carry)
    return step(pl.multiple_of(past + qi * blk, blk), blk, carry, True)


def _fox_attn_online_kernel(q_ref, qa_ref, k_ref, ka_ref, v_ref, o_ref, *, blk, pblk, past, nq):
    qops, low = _fox_query_operands(q_ref, qa_ref, blk)
    causal = _tri(blk)

    def step(start, size, carry, masked):
        rows = pl.ds(start, size)
        kop = jnp.concatenate([k_ref[0, rows, :], ka_ref[0, rows, :]], axis=1)
        v = v_ref[0, rows, :]
        out = []
        for s in range(2):
            m, l, acc = carry[s]
            sc = _dot_nt(qops[s], kop)
            if masked:
                sc = jnp.where(causal, sc, -jnp.inf)
            m_new = jnp.maximum(m, jnp.max(sc, axis=-1, keepdims=True))
            alpha = jnp.exp2(m - m_new)
            p = jnp.exp2(sc - m_new)
            l = alpha * l + jnp.sum(p, axis=-1, keepdims=True)
            acc = alpha * acc + _dot(p.astype(BF16), v)
            out.append((m_new, l, acc))
        return tuple(out)

    init = tuple((jnp.full((blk, 1), -jnp.inf, F32), jnp.zeros((blk, 1), F32),
                  jnp.zeros((blk, LANES), F32)) for _ in range(2))
    (_, l0, a0), (_, l1, a1) = _fox_key_blocks(step, init, blk, pblk, past, nq)
    o_ref[0] = jnp.where(low, a0 / l0, a1 / l1).astype(BF16)


def _fox_attn_bounded_kernel(q_ref, qa_ref, k_ref, ka_ref, vt_ref, o_ref, qop_ref, acc_ref, *,
                             blk, pblk, past, nq):
    pair = pl.program_id(1)
    lane = lax.broadcasted_iota(jnp.int32, (blk, LANES), 1)
    q = q_ref[0]
    qa = qa_ref[0]
    for s in range(2):
        lo = (2 * pair + s) * _AUG
        head_lanes = (lane < FOX_DH) if s == 0 else (lane >= FOX_DH)
        qop_ref[s * blk:(s + 1) * blk, 0:LANES] = jnp.where(head_lanes, q, jnp.zeros_like(q))
        qop_ref[s * blk:(s + 1) * blk, LANES:2 * LANES] = jnp.where(
            (lane >= lo) & (lane < lo + _AUG), qa, jnp.zeros_like(qa))
    acc_ref[...] = jnp.zeros_like(acc_ref)
    r = lax.broadcasted_iota(jnp.int32, (blk, blk), 0)
    c = lax.broadcasted_iota(jnp.int32, (blk, blk), 1)
    causal = r <= c
    chan = lax.broadcasted_iota(jnp.int32, (LANES, 1), 0) < FOX_DH

    def accumulate(start, size, qop, cols, mask):
        rows = pl.ds(start, size)
        kop = jnp.concatenate([k_ref[0, rows, :], ka_ref[0, rows, :]], axis=1)
        vt = vt_ref[0, :, rows]
        ones = jnp.ones_like(vt)
        sc = _dot_nt(kop, qop)
        ncols = sc.shape[1] // 2
        s0, s1 = sc[:, :ncols], sc[:, ncols:]
        if mask is not None:
            s0 = jnp.where(mask, s0, -jnp.inf)
            s1 = jnp.where(mask, s1, -jnp.inf)
        acc_ref[0, :, cols] += _dot(jnp.where(chan, vt, ones), jnp.exp2(s0).astype(BF16))
        acc_ref[1, :, cols] += _dot(jnp.where(chan, ones, vt), jnp.exp2(s1).astype(BF16))

    half = blk // 2

    def step(start, size, carry, masked):
        if not masked:
            accumulate(start, size, qop_ref[...], slice(None), None)
        elif half % LANES:
            accumulate(start, size, qop_ref[...], slice(None), causal)
        else:
            accumulate(start, half, qop_ref[...], slice(None), causal[:half])
            late = jnp.concatenate([qop_ref[half:blk], qop_ref[blk + half:2 * blk]], axis=0)
            accumulate(start + half, half, late, slice(half, blk), causal[:half, :half])
        return carry

    _fox_key_blocks(step, 0, blk, pblk, past, nq)
    a0 = acc_ref[0]
    a1 = acc_ref[1]
    out_t = jnp.concatenate([a0[:FOX_DH] / a0[FOX_DH:], a1[FOX_DH:] / a1[:FOX_DH]], axis=0)
    o_ref[0] = out_t.T.astype(BF16)


def _fox_attn(body, transposed_v, qs, qa, kb, ka, v, bsz, seq, past, blk, pblk):
    scratch = ([pltpu.VMEM((2 * blk, 2 * LANES), BF16), pltpu.VMEM((2, LANES, blk), F32)]
               if transposed_v else [])
    tk = past + seq
    d = qs.shape[-1]
    pb = past // blk
    qspec = pl.BlockSpec((1, blk, LANES), lambda b, p, i: (b, i, p))
    kspec = pl.BlockSpec((1, tk, LANES), lambda b, p, i: (b, 0, p))
    vspec = pl.BlockSpec((1, LANES, tk), lambda b, p, i: (b, p, 0)) if transposed_v else kspec
    return pl.pallas_call(
        functools.partial(body, blk=blk, pblk=pblk, past=past, nq=seq // blk),
        grid=(bsz, d // LANES, seq // blk),
        in_specs=[qspec,
                  pl.BlockSpec((1, blk, LANES), lambda b, p, i: (b, pb + i, 0)),
                  kspec,
                  pl.BlockSpec((1, tk, LANES), lambda b, p, i: (b, 0, 0)),
                  vspec],
        out_specs=qspec,
        out_shape=jax.ShapeDtypeStruct((bsz, seq, d), BF16),
        scratch_shapes=scratch,
        compiler_params=_cparams("parallel", "parallel", "arbitrary"),
        name=body.__name__.strip("_").replace("_kernel", ""),
    )(qs, qa, kb, ka, v)


def _trunk(x, pos0, hg_s, ret_s, conv_s, fox_k, fox_v, fox_lf, W):
    bsz, seq, d = x.shape
    m = bsz * seq
    past = fox_k.shape[1]
    tm = min(seq, 256)
    tm_proj = min(seq, 512)
    tm_ffn = min(m, 512)
    x = x.reshape(m, d)

    def ffn(x, a, w_out, layer):
        return _outproj_ffn(x, a, w_out, W["norm_ffn"][layer], W["ffn_w_g"][layer],
                            W["ffn_w_u"][layer], W["ffn_w_d"][layer], tm_ffn, FFN_BOUNDS)

    proj = _norm_proj(x, W["norm_mix"][0], W["hg_w_in"], tm_proj, 512)
    a, hg_s = _hgrn_core(proj, W["hg_lower_bounds"], W["hg_out_gain"], hg_s, 0, bsz, seq, tm)
    x = ffn(x, a, W["hg_w_out"], 0)

    inv = jnp.power(ROPE_BASE, -jnp.arange(0, RET_DK, 2, dtype=F32) / RET_DK)
    cos, sin = _rope_table(inv, pos0, seq, tm)
    chunk = min(seq, 256)
    parts = _ret_proj(x, W["norm_mix"][1], W["ret_w_in"], cos, sin, seq, tm_proj, chunk)
    a, ret_s = _ret_core(*parts, W["ret_gn_gain"], ret_s, bsz, seq, tm, chunk)
    x = ffn(x, a, W["ret_w_out"], 1)

    a, conv_s = _conv_core(x, W["norm_mix"][2], W["conv_w_in"], W["conv_w"], conv_s, bsz, seq,
                            tm_proj)
    x = ffn(x, a, W["conv_w_out"], 2)

    qs, kb, vb, vt, lf, kn, v = _fox_prep(x, W["norm_mix"][3], W["fox_w_qkv"], W["fox_w_f"],
                                          W["fox_b_f"], W["fox_q_gain"], W["fox_k_gain"], bsz,
                                          tm_proj)
    blk = min(seq, 1024)
    lf3 = lf.reshape(bsz, seq, LANES)
    kb3 = kb.reshape(bsz, seq, d)
    vb3 = vb.reshape(bsz, seq, d)
    if past:
        lf3 = jnp.concatenate([jnp.pad(fox_lf, ((0, 0), (0, 0), (0, LANES - FOX_HEADS))), lf3], axis=1)
        kb3 = jnp.concatenate([fox_k.reshape(bsz, past, d).astype(BF16), kb3], axis=1)
        past_v = fox_v.reshape(bsz, past, d).astype(BF16)
        vb3 = jnp.concatenate([past_v, vb3], axis=1)
        vt = jnp.concatenate([past_v.transpose(0, 2, 1), vt], axis=2)
    shift, bounded = _fox_score_bound(W["fox_q_gain"], W["fox_k_gain"], fox_lf)
    qa, ka = _fox_gate(lf3, shift)
    attn = functools.partial(_fox_attn, bsz=bsz, seq=seq, past=past, blk=blk,
                             pblk=min(past, 512) if past else blk)
    q3 = qs.reshape(bsz, seq, d)
    a = lax.cond(bounded,
                 lambda: attn(_fox_attn_bounded_kernel, True, q3, qa, kb3, ka, vt),
                 lambda: attn(_fox_attn_online_kernel, False, q3, qa, kb3, ka, vb3))
    x = ffn(x, a.reshape(m, d), W["fox_w_out"], 3)

    return (x.reshape(bsz, seq, d), hg_s, ret_s, conv_s,
            kn.reshape(bsz, seq, FOX_HEADS, FOX_DH), v.reshape(bsz, seq, FOX_HEADS, FOX_DH),
            lf[:, :FOX_HEADS].reshape(bsz, seq, FOX_HEADS))


def kernel(x_prompt, x_sample, state_hgrn, state_ret, state_conv, cache_fox_k, cache_fox_v, cache_fox_logf, norm_mix, norm_ffn, hg_w_in, hg_lower_bounds, hg_out_gain, hg_w_out, ret_w_in, ret_gn_gain, ret_w_out, conv_w_in, conv_w, conv_w_out, fox_w_in, fox_b_f, fox_q_gain, fox_k_gain, fox_w_out, ffn_w_gu, ffn_w_down):
    d = D_MODEL
    bf = lambda w: w.astype(BF16)
    W = dict(
        norm_mix=norm_mix, norm_ffn=norm_ffn,
        hg_w_in=bf(hg_w_in), hg_lower_bounds=hg_lower_bounds, hg_out_gain=hg_out_gain,
        hg_w_out=bf(hg_w_out),
        ret_w_in=bf(ret_w_in), ret_gn_gain=ret_gn_gain, ret_w_out=bf(ret_w_out),
        conv_w_in=bf(conv_w_in), conv_w=conv_w, conv_w_out=bf(conv_w_out),
        fox_w_qkv=bf(fox_w_in[:, :3 * d]),
        fox_w_f=bf(jnp.pad(fox_w_in[:, 3 * d:], ((0, 0), (0, LANES - FOX_HEADS)))),
        fox_b_f=jnp.pad(fox_b_f, (0, LANES - FOX_HEADS)).reshape(1, LANES),
        fox_q_gain=fox_q_gain, fox_k_gain=fox_k_gain, fox_w_out=bf(fox_w_out),
        ffn_w_g=bf(ffn_w_gu[:, :, :D_FF]), ffn_w_u=bf(ffn_w_gu[:, :, D_FF:]), ffn_w_d=bf(ffn_w_down),
    )
    bsz = x_prompt.shape[0]
    dt = x_prompt.dtype
    (y_p, hg_p, ret_p, conv_p, fk_p, fv_p, flf_p) = _trunk(
        x_prompt, 0,
        jnp.zeros((bsz, HG_HEADS, HG_DK, HG_DV), F32),
        jnp.zeros((bsz, RET_HEADS, RET_DK, RET_DV), F32),
        jnp.zeros((bsz, CONV_WIDTH - 1, d), dt),
        jnp.zeros((bsz, 0, FOX_HEADS, FOX_DH), dt),
        jnp.zeros((bsz, 0, FOX_HEADS, FOX_DH), dt),
        jnp.zeros((bsz, 0, FOX_HEADS), F32), W)
    past = cache_fox_k.shape[1]
    (y_s, hg_s, ret_s, conv_s, fk_s, fv_s, flf_s) = _trunk(
        x_sample, past, state_hgrn, state_ret, state_conv,
        cache_fox_k, cache_fox_v, cache_fox_logf, W)
    return (y_p, y_s, hg_p, hg_s, ret_p, ret_s, conv_p, conv_s,
            fk_p, fv_p, flf_p, fk_s, fv_s, flf_s)
```

```python
import functools
import math

import numpy as np
import jax
import jax.numpy as jnp
from jax import lax
from jax.experimental import pallas as pl
from jax.experimental.pallas import tpu as pltpu

F32 = jnp.float32
BF16 = jnp.bfloat16

D_MODEL = 1024
EPS = 1e-6
HG_HEADS, HG_DK, HG_DV = 8, 128, 128
HG_CHUNK = 64
RET_HEADS, RET_DK, RET_DV = 4, 256, 512
ROPE_BASE = 10000.0
CONV_WIDTH = 3
FOX_HEADS, FOX_DH = 16, 64
D_FF = 2816
MXU_DIM = 256
FFN_BOUNDS = (0, (D_FF // MXU_DIM + 1) // 2 * MXU_DIM, D_FF)
LANES = 128

VMEM_LIMIT = 56 * 1024 * 1024


def _cparams(*sem):
    return pltpu.CompilerParams(dimension_semantics=sem, vmem_limit_bytes=VMEM_LIMIT)


def _resident(shape):
    nd = len(shape)
    return pl.BlockSpec(shape, lambda *_: (0,) * nd, pipeline_mode=pl.Buffered(1))


def _dot(a, b):
    return jnp.dot(a, b, preferred_element_type=F32)


def _dot_nt(a, b):
    return lax.dot_general(a, b, (((1,), (1,)), ((), ())), preferred_element_type=F32)


def _dot_tn(a, b):
    return lax.dot_general(a, b, (((0,), (0,)), ((), ())), preferred_element_type=F32)


def _sigmoid(x):
    return 0.5 * jnp.tanh(0.5 * x) + 0.5


def _rms(x, g):
    return x * lax.rsqrt(jnp.mean(x * x, axis=-1, keepdims=True) + EPS) * g


def _split3(x):
    hi = x.astype(BF16)
    r = x - hi.astype(F32)
    mid = r.astype(BF16)
    lo = (r - mid.astype(F32)).astype(BF16)
    return hi, mid, lo


def _tri(n):
    r = lax.broadcasted_iota(jnp.int32, (n, n), 0)
    c = lax.broadcasted_iota(jnp.int32, (n, n), 1)
    return r >= c


def _cumsum_rows(tri_bf16, x):
    hi, mid, lo = _split3(x)
    return _dot(tri_bf16, hi) + _dot(tri_bf16, mid) + _dot(tri_bf16, lo)


def _norm_proj_kernel(x_ref, g_ref, w_ref, o_ref, *, tn):
    h = _rms(x_ref[...], g_ref[...]).astype(BF16)
    for c in range(o_ref.shape[1] // tn):
        o_ref[:, c * tn:(c + 1) * tn] = _dot(h, w_ref[:, c * tn:(c + 1) * tn])


def _norm_proj(x, g, w, tm, tn):
    m, d = x.shape
    n = w.shape[1]
    return pl.pallas_call(
        functools.partial(_norm_proj_kernel, tn=tn),
        grid=(m // tm,),
        in_specs=[pl.BlockSpec((tm, d), lambda i: (i, 0)), _resident((1, d)), _resident((d, n))],
        out_specs=pl.BlockSpec((tm, n), lambda i: (i, 0)),
        out_shape=jax.ShapeDtypeStruct((m, n), F32),
        compiler_params=_cparams("parallel"),
        name="norm_proj",
    )(x, g.reshape(1, d), w)


def _ffn_kernel(x_ref, a_ref, wo_ref, g_ref, wg_ref, wu_ref, wd_ref, y_ref, *, bounds):
    tm = x_ref.shape[0]
    groups = [slice(0, tm // 2), slice(tm // 2, tm)] if tm % 32 == 0 else [slice(0, tm)]
    x1 = [x_ref[r, :] + _dot(a_ref[r, :], wo_ref[...]) for r in groups]
    h = [_rms(x, g_ref[...]).astype(BF16) for x in x1]
    acc = list(x1)
    for lo, hi in zip(bounds[:-1], bounds[1:]):
        gate = [_dot(hh, wg_ref[:, lo:hi]) for hh in h]
        up = [_dot(hh, wu_ref[:, lo:hi]) for hh in h]
        act = [(g * _sigmoid(g) * u).astype(BF16) for g, u in zip(gate, up)]
        acc = [a + _dot(t, wd_ref[lo:hi, :]) for a, t in zip(acc, act)]
    for r, a in zip(groups, acc):
        y_ref[r, :] = a


def _outproj_ffn(x, a, w_out, g, w_g, w_u, w_d, tm, bounds):
    m, d = x.shape
    ka = a.shape[1]
    return pl.pallas_call(
        functools.partial(_ffn_kernel, bounds=bounds),
        grid=(m // tm,),
        in_specs=[pl.BlockSpec((tm, d), lambda i: (i, 0)),
                  pl.BlockSpec((tm, ka), lambda i: (i, 0)),
                  _resident((ka, d)), _resident((1, d)),
                  _resident((d, D_FF)), _resident((d, D_FF)), _resident((D_FF, d))],
        out_specs=pl.BlockSpec((tm, d), lambda i: (i, 0)),
        out_shape=jax.ShapeDtypeStruct((m, d), F32),
        compiler_params=_cparams("parallel"),
        name="outproj_ffn",
    )(x, a, w_out, g.reshape(1, d), w_g, w_u, w_d)


def _hgrn_kernel(p_ref, lbw_ref, gain_ref, s0_ref, o_ref, sout_ref, st_ref, *, layer, nchunk):
    t = pl.program_id(1)
    L = HG_CHUNK
    nk = HG_HEADS * HG_DK

    @pl.when(t == 0)
    def _():
        for h in range(HG_HEADS):
            st_ref[h] = s0_ref[0, h].T

    lbw = lbw_ref[...]
    e = jnp.exp(lbw - jnp.max(lbw, axis=0, keepdims=True))
    lb = jnp.sum(e[:layer + 1], axis=0, keepdims=True) / jnp.sum(e, axis=0, keepdims=True)
    gain = gain_ref[...]
    tm = p_ref.shape[0]

    r = lax.broadcasted_iota(jnp.int32, (tm, tm), 0)
    c = lax.broadcasted_iota(jnp.int32, (tm, tm), 1)
    shift = L.bit_length() - 1
    tri = (r >= c) & ((r >> shift) == (c >> shift))
    q = p_ref[:, 0:nk]
    fg = lb + (1.0 - lb) * _sigmoid(p_ref[:, nk:2 * nk])
    k = 1.0 - fg
    G = _cumsum_rows(tri.astype(BF16), jnp.log(fg))

    def per_chunk_row(offset):
        return jnp.concatenate(
            [jnp.broadcast_to(G[i * L + offset:i * L + offset + 1], (L, nk)) for i in range(nchunk)],
            axis=0)

    g_mid = per_chunk_row(L // 2)
    g_last = per_chunk_row(L - 1)
    qa = (q * jnp.exp(G - g_mid)).astype(BF16)
    kb = (k * jnp.exp(g_mid - G)).astype(BF16)
    qs = (q * jnp.exp(G)).astype(BF16)
    kl = (k * jnp.exp(g_last - G)).astype(BF16)
    heads = range(HG_HEADS)
    sls = [slice(h * HG_DK, (h + 1) * HG_DK) for h in heads]
    vs = [p_ref[:, 2 * nk + h * HG_DV:2 * nk + (h + 1) * HG_DV].astype(BF16) for h in heads]
    scs = [jnp.where(tri, _dot_nt(qa[:, sl], kb[:, sl]), 0.0).astype(BF16) for sl in sls]
    o_intra = [_dot(sc, v) for sc, v in zip(scs, vs)]
    us = [[_dot_tn(v[i * L:(i + 1) * L], kl[i * L:(i + 1) * L, sl]) for i in range(nchunk)]
          for v, sl in zip(vs, sls)]
    sts = [st_ref[h] for h in heads]
    outs = [[] for _ in heads]
    for i in range(nchunk):
        rows = slice(i * L, (i + 1) * L)
        for h in heads:
            outs[h].append(o_intra[h][rows] + _dot_nt(qs[rows, sls[h]], sts[h].astype(BF16)))
            sts[h] = sts[h] * jnp.exp(G[(i + 1) * L - 1:(i + 1) * L, sls[h]]) + us[h][i]
    for h in heads:
        st_ref[h] = sts[h]
        gate = p_ref[:, 3 * nk + h * HG_DV:3 * nk + (h + 1) * HG_DV]
        o_ref[:, h * HG_DV:(h + 1) * HG_DV] = (
            _rms(jnp.concatenate(outs[h], axis=0), gain) * (gate * _sigmoid(gate))).astype(BF16)

    @pl.when(t == pl.num_programs(1) - 1)
    def _():
        for h in range(HG_HEADS):
            sout_ref[0, h] = st_ref[h].T


def _hgrn_core(proj, lower_bounds, out_gain, state, layer, bsz, seq, tm):
    nt = seq // tm
    w = proj.shape[1]
    st_shape = (1, HG_HEADS, HG_DK, HG_DV)
    return pl.pallas_call(
        functools.partial(_hgrn_kernel, layer=layer, nchunk=tm // HG_CHUNK),
        grid=(bsz, nt),
        in_specs=[pl.BlockSpec((tm, w), lambda b, t: (b * nt + t, 0)),
                  _resident(lower_bounds.shape), _resident((1, HG_DV)),
                  pl.BlockSpec(st_shape, lambda b, t: (b, 0, 0, 0))],
        out_specs=[pl.BlockSpec((tm, HG_HEADS * HG_DV), lambda b, t: (b * nt + t, 0)),
                   pl.BlockSpec(st_shape, lambda b, t: (b, 0, 0, 0))],
        out_shape=[jax.ShapeDtypeStruct((bsz * seq, HG_HEADS * HG_DV), BF16),
                   jax.ShapeDtypeStruct((bsz,) + st_shape[1:], F32)],
        scratch_shapes=[pltpu.VMEM((HG_HEADS, HG_DV, HG_DK), F32)],
        compiler_params=_cparams("parallel", "arbitrary"),
        name="hgrn_core",
    )(proj, lower_bounds, out_gain.reshape(1, HG_DV), state)


def _rope_table_kernel(inv_ref, cos_ref, sin_ref, *, pos0):
    tb = cos_ref.shape[0]
    pos = (pos0 + pl.program_id(0) * tb
           + lax.broadcasted_iota(jnp.int32, cos_ref.shape, 0)).astype(F32)
    ang = pos * inv_ref[...]
    cos_ref[...] = jnp.cos(ang)
    sin_ref[...] = jnp.sin(ang)


def _rope_table(inv, pos0, seq, tb):
    half = inv.shape[0]
    spec = pl.BlockSpec((tb, half), lambda i: (i, 0))
    return pl.pallas_call(
        functools.partial(_rope_table_kernel, pos0=pos0),
        grid=(seq // tb,),
        in_specs=[_resident((1, half))],
        out_specs=[spec, spec],
        out_shape=[jax.ShapeDtypeStruct((seq, half), F32)] * 2,
        compiler_params=_cparams("parallel"),
        name="rope_table",
    )(inv.reshape(1, half))


def _ret_proj_kernel(x_ref, g_ref, w_ref, cos_ref, sin_ref, qr_ref, kb_ref, kd_ref, v_ref, sg_ref,
                     *, L):
    tm = x_ref.shape[0]
    nq = RET_HEADS * RET_DK
    nv = RET_HEADS * RET_DV
    half = RET_DK // 2
    hx = _rms(x_ref[...], g_ref[...]).astype(BF16)
    cos = cos_ref[...]
    sin = sin_ref[...]
    ti = (lax.broadcasted_iota(jnp.int32, (tm, 1), 0) % L).astype(F32)
    scale = RET_DK ** -0.5

    def rope(a):
        a1, a2 = a[:, :half], a[:, half:]
        return jnp.concatenate([a1 * cos - a2 * sin, a1 * sin + a2 * cos], axis=1)

    for h in range(RET_HEADS):
        lg = math.log(1.0 - 2.0 ** (-5.0 - h))
        cq = slice(h * RET_DK, (h + 1) * RET_DK)
        cv = slice(h * RET_DV, (h + 1) * RET_DV)
        qr_ref[:, cq] = rope(_dot(hx, w_ref[:, cq])).astype(BF16)
        kr = rope(_dot(hx, w_ref[:, nq + h * RET_DK:nq + (h + 1) * RET_DK])) * scale
        kb_ref[:, cq] = kr.astype(BF16)
        kd_ref[:, cq] = (kr * jnp.exp(lg * (L - 1.0 - ti))).astype(BF16)
        v_ref[:, cv] = _dot(hx, w_ref[:, 2 * nq + h * RET_DV:2 * nq + (h + 1) * RET_DV]).astype(BF16)
        gate = _dot(hx, w_ref[:, 2 * nq + nv + h * RET_DV:2 * nq + nv + (h + 1) * RET_DV])
        sg_ref[:, cv] = gate * _sigmoid(gate)


def _ret_proj(x, g, w, cos, sin, seq, tm, chunk):
    m, d = x.shape
    nq = RET_HEADS * RET_DK
    nv = RET_HEADS * RET_DV
    half = RET_DK // 2
    ntp = seq // tm
    row = lambda n: pl.BlockSpec((tm, n), lambda i: (i, 0))
    pos = pl.BlockSpec((tm, half), lambda i: (i % ntp, 0))
    return pl.pallas_call(
        functools.partial(_ret_proj_kernel, L=chunk),
        grid=(m // tm,),
        in_specs=[row(d), _resident((1, d)), _resident(w.shape), pos, pos],
        out_specs=[row(nq), row(nq), row(nq), row(nv), row(nv)],
        out_shape=[jax.ShapeDtypeStruct((m, nq), BF16)] * 3
        + [jax.ShapeDtypeStruct((m, nv), BF16), jax.ShapeDtypeStruct((m, nv), F32)],
        compiler_params=_cparams("parallel"),
        name="ret_proj",
    )(x, g.reshape(1, d), w, cos, sin)


def _ret_kernel(qr_ref, kb_ref, kd_ref, v_ref, sg_ref, decay_ref, gn_ref, s0_ref, o_ref, sout_ref,
                st_ref, *, L, nchunk):
    t = pl.program_id(1)

    @pl.when(t == 0)
    def _():
        st_ref[...] = s0_ref[0]

    ti = lax.broadcasted_iota(jnp.int32, (L, 1), 0).astype(F32)

    def chunk(c, carry):
        rows = pl.ds(pl.multiple_of(c * L, L), L)
        heads = range(RET_HEADS)
        lgs = [math.log(1.0 - 2.0 ** (-5.0 - h)) for h in heads]
        cqs = [slice(h * RET_DK, (h + 1) * RET_DK) for h in heads]
        cvs = [slice(h * RET_DV, (h + 1) * RET_DV) for h in heads]
        qrs = [qr_ref[rows, cq] for cq in cqs]
        vs = [v_ref[rows, cv] for cv in cvs]
        scs = [(_dot_nt(qrs[h], kb_ref[rows, cqs[h]]) * decay_ref[h]).astype(BF16) for h in heads]
        sts = [st_ref[h] for h in heads]
        os = [_dot(scs[h], vs[h]) + _dot(qrs[h], sts[h].astype(BF16)) * jnp.exp(lgs[h] * (ti + 1.0))
              for h in heads]
        for h in heads:
            st_ref[h] = math.exp(lgs[h] * L) * sts[h] + _dot_tn(kd_ref[rows, cqs[h]], vs[h])
        for h in heads:
            o = os[h]
            mu = jnp.mean(o, axis=-1, keepdims=True)
            d = o - mu
            var = jnp.mean(d * d, axis=-1, keepdims=True)
            on = d * lax.rsqrt(var + EPS) * gn_ref[:, cvs[h]]
            o_ref[rows, cvs[h]] = (on * sg_ref[rows, cvs[h]]).astype(BF16)
        return carry

    lax.fori_loop(0, nchunk, chunk, 0)

    @pl.when(t == pl.num_programs(1) - 1)
    def _():
        sout_ref[0] = st_ref[...]


def _ret_core(qr, kb, kd, v, sg, gn_gain, state, bsz, seq, tm, chunk):
    lg = jnp.log(1.0 - jnp.power(2.0, -5.0 - jnp.arange(RET_HEADS, dtype=F32)))[:, None, None]
    idx = jnp.arange(chunk, dtype=F32)
    diff = idx[:, None] - idx[None, :]
    decay = jnp.where(diff >= 0, jnp.exp(lg * jnp.maximum(diff, 0.0)), 0.0)
    nt = seq // tm
    nq = RET_HEADS * RET_DK
    nv = RET_HEADS * RET_DV
    st_shape = (1, RET_HEADS, RET_DK, RET_DV)
    row = lambda n: pl.BlockSpec((tm, n), lambda b, t: (b * nt + t, 0))
    return pl.pallas_call(
        functools.partial(_ret_kernel, L=chunk, nchunk=tm // chunk),
        grid=(bsz, nt),
        in_specs=[row(nq), row(nq), row(nq), row(nv), row(nv), _resident(decay.shape),
                  _resident((1, nv)), pl.BlockSpec(st_shape, lambda b, t: (b, 0, 0, 0))],
        out_specs=[row(nv), pl.BlockSpec(st_shape, lambda b, t: (b, 0, 0, 0))],
        out_shape=[jax.ShapeDtypeStruct((bsz * seq, nv), BF16),
                   jax.ShapeDtypeStruct((bsz,) + st_shape[1:], F32)],
        scratch_shapes=[pltpu.VMEM((RET_HEADS, RET_DK, RET_DV), F32)],
        compiler_params=_cparams("parallel", "arbitrary"),
        name="ret_core",
    )(qr, kb, kd, v, sg, decay, gn_gain.reshape(1, nv), state)


_CONV_PAD = 8


def _conv_kernel(x_ref, g_ref, w_ref, cw_ref, s0_ref, a_ref, sout_ref, z_ref):
    t = pl.program_id(1)
    tm, d = x_ref.shape
    nc = CONV_WIDTH - 1

    @pl.when(t == 0)
    def _():
        z_ref[_CONV_PAD - nc:_CONV_PAD, :] = s0_ref[0]

    h = _rms(x_ref[...], g_ref[...]).astype(BF16)
    b = _dot(h, w_ref[:, 0:d])
    z_ref[_CONV_PAD:_CONV_PAD + tm, :] = _dot(h, w_ref[:, d:2 * d]) * _dot(h, w_ref[:, 2 * d:3 * d])
    y = cw_ref[0:1, :] * z_ref[_CONV_PAD - nc:_CONV_PAD - nc + tm, :]
    for j in range(1, CONV_WIDTH):
        y = y + cw_ref[j:j + 1, :] * z_ref[_CONV_PAD - nc + j:_CONV_PAD - nc + j + tm, :]
    a_ref[...] = (b * y).astype(BF16)
    last = z_ref[_CONV_PAD + tm - nc:_CONV_PAD + tm, :]
    z_ref[_CONV_PAD - nc:_CONV_PAD, :] = last

    @pl.when(t == pl.num_programs(1) - 1)
    def _():
        sout_ref[0] = last


def _conv_core(x, g, w_in, conv_w, state, bsz, seq, tm):
    nt = seq // tm
    d = x.shape[1]
    nc = CONV_WIDTH - 1
    return pl.pallas_call(
        _conv_kernel,
        grid=(bsz, nt),
        in_specs=[pl.BlockSpec((tm, d), lambda b, t: (b * nt + t, 0)),
                  _resident((1, d)), _resident((d, 3 * d)), _resident((CONV_WIDTH, d)),
                  pl.BlockSpec((1, nc, d), lambda b, t: (b, 0, 0))],
        out_specs=[pl.BlockSpec((tm, d), lambda b, t: (b * nt + t, 0)),
                   pl.BlockSpec((1, nc, d), lambda b, t: (b, 0, 0))],
        out_shape=[jax.ShapeDtypeStruct((bsz * seq, d), BF16),
                   jax.ShapeDtypeStruct((bsz, nc, d), F32)],
        scratch_shapes=[pltpu.VMEM((_CONV_PAD + tm, d), F32)],
        compiler_params=_cparams("parallel", "arbitrary"),
        name="conv_core",
    )(x, g.reshape(1, d), w_in, conv_w, state)


_AUG = 6
LOG2E = 1.4426950408889634


def _fox_prep_kernel(x_ref, g_ref, w_ref, wf_ref, bf_ref, qg_ref, kg_ref, sel_ref, selt_ref,
                     qs_ref, kb_ref, vb_ref, vt_ref, lf_ref, kn_hbm, v_hbm, kscr, vscr, ksem, vsem):
    i = pl.program_id(0)
    n = pl.num_programs(0)
    tm, d = x_ref.shape
    slot = i % 2
    h = _rms(x_ref[...], g_ref[...]).astype(BF16)

    def head_norm(a, gain):
        ms = _dot((a * a).astype(BF16), sel_ref[...]) * (1.0 / FOX_DH)
        rs = lax.rsqrt(ms + EPS)
        hi = rs.astype(BF16)
        lo = (rs - hi.astype(F32)).astype(BF16)
        return a * (_dot(hi, selt_ref[...]) + _dot(lo, selt_ref[...])) * gain

    def head_copies(scr, hbm, sem, s, step):
        return [pltpu.make_async_copy(scr.at[s, hd], hbm.at[pl.ds(step * tm, tm), hd, :], sem.at[s])
                for hd in range(FOX_HEADS)]

    def emit_heads(val, scr, hbm, sem):
        @pl.when(i >= 2)
        def _():
            for c in head_copies(scr, hbm, sem, slot, i - 2):
                c.wait()
        for hd in range(FOX_HEADS):
            scr[slot, hd] = val[:, hd * FOX_DH:(hd + 1) * FOX_DH]
        for c in head_copies(scr, hbm, sem, slot, i):
            c.start()

    qn = head_norm(_dot(h, w_ref[:, 0:d]), qg_ref[...])
    qs_ref[...] = (qn * (FOX_DH ** -0.5 * LOG2E)).astype(BF16)
    kn = head_norm(_dot(h, w_ref[:, d:2 * d]), kg_ref[...])
    emit_heads(kn, kscr, kn_hbm, ksem)
    kb_ref[...] = kn.astype(BF16)
    v = _dot(h, w_ref[:, 2 * d:3 * d])
    emit_heads(v, vscr, v_hbm, vsem)
    vb_ref[...] = v.astype(BF16)
    vt_ref[0] = v.T.astype(BF16)
    fl = _dot(h, wf_ref[...]) + bf_ref[...]
    ls = jnp.minimum(fl, 0.0) - jnp.log(1.0 + jnp.exp(-jnp.abs(fl)))
    lf_ref[...] = ls[:, :FOX_HEADS]

    @pl.when(i == n - 1)
    def _():
        for scr, hbm, sem in ((kscr, kn_hbm, ksem), (vscr, v_hbm, vsem)):
            for c in head_copies(scr, hbm, sem, slot, i):
                c.wait()

    @pl.when((i == n - 1) & (i >= 1))
    def _():
        for scr, hbm, sem in ((kscr, kn_hbm, ksem), (vscr, v_hbm, vsem)):
            for c in head_copies(scr, hbm, sem, 1 - slot, i - 1):
                c.wait()


def _fox_prep(x, g, w_qkv, w_f, b_f, q_gain, k_gain, bsz, tm):
    m, d = x.shape
    nt = m // bsz // tm
    col = np.arange(d) // FOX_DH
    sel = (col[:, None] == np.arange(LANES)[None, :]).astype(np.float32)
    row = pl.BlockSpec((tm, d), lambda i: (i, 0))
    heads = jax.ShapeDtypeStruct((m, FOX_HEADS, FOX_DH), F32)
    return pl.pallas_call(
        _fox_prep_kernel,
        grid=(m // tm,),
        in_specs=[row, _resident((1, d)), _resident((d, 3 * d)), _resident((d, LANES)),
                  _resident((1, LANES)), _resident((1, d)), _resident((1, d)),
                  _resident((d, LANES)), _resident((LANES, d))],
        out_specs=[row, row, row,
                   pl.BlockSpec((1, d, tm), lambda i: (i // nt, 0, i % nt)),
                   pl.BlockSpec((tm, FOX_HEADS), lambda i: (i, 0)),
                   pl.BlockSpec(memory_space=pl.ANY), pl.BlockSpec(memory_space=pl.ANY)],
        out_shape=[jax.ShapeDtypeStruct((m, d), BF16), jax.ShapeDtypeStruct((m, d), BF16),
                   jax.ShapeDtypeStruct((m, d), BF16),
                   jax.ShapeDtypeStruct((bsz, d, m // bsz), BF16),
                   jax.ShapeDtypeStruct((m, FOX_HEADS), F32), heads, heads],
        scratch_shapes=[pltpu.VMEM((2, FOX_HEADS, tm, FOX_DH), F32),
                        pltpu.VMEM((2, FOX_HEADS, tm, FOX_DH), F32),
                        pltpu.SemaphoreType.DMA((2,)), pltpu.SemaphoreType.DMA((2,))],
        compiler_params=_cparams("arbitrary"),
        name="fox_prep",
    )(x, g.reshape(1, d), w_qkv, w_f, b_f, jnp.tile(q_gain, FOX_HEADS).reshape(1, d),
      jnp.tile(k_gain, FOX_HEADS).reshape(1, d), jnp.asarray(sel, BF16), jnp.asarray(sel.T, BF16))


def _fox_gate_kernel(lf_ref, shift_ref, pq_ref, pk_ref, cq_ref, ck_ref, qa_ref, ka_ref, carry_ref):
    t = pl.program_id(1)
    cb = lf_ref.shape[1]

    @pl.when(t == 0)
    def _():
        carry_ref[...] = jnp.zeros_like(carry_ref)

    F = _cumsum_rows(_tri(cb).astype(BF16), lf_ref[0]) + carry_ref[...]
    carry_ref[...] = F[cb - 1:cb]
    F2 = F * LOG2E
    qa = cq_ref[...]
    ka = ck_ref[...]
    for j, (pq, pk) in enumerate(zip(_split3(F2 - shift_ref[...]), _split3(F2))):
        qa = qa + _dot(pq, pq_ref[j])
        ka = ka + _dot(pk, pk_ref[j])
    qa_ref[0] = qa.astype(BF16)
    ka_ref[0] = ka.astype(BF16)


def _fox_gate(lf_all, shift):
    bsz, tk, _ = lf_all.shape
    cb = max(c for c in range(64, 769, 64) if tk % c == 0)
    place_q = np.zeros((3, FOX_HEADS, LANES), np.float32)
    place_k = np.zeros((3, FOX_HEADS, LANES), np.float32)
    const_q = np.zeros((1, LANES), np.float32)
    const_k = np.zeros((1, LANES), np.float32)
    for h in range(FOX_HEADS):
        for j in range(3):
            place_q[j, h, h * _AUG + j] = 1.0
            place_k[j, h, h * _AUG + 3 + j] = -1.0
            const_q[0, h * _AUG + 3 + j] = 1.0
            const_k[0, h * _AUG + j] = 1.0
    spec = pl.BlockSpec((1, cb, LANES), lambda b, t: (b, t, 0))
    return pl.pallas_call(
        _fox_gate_kernel,
        grid=(bsz, tk // cb),
        in_specs=[pl.BlockSpec((1, cb, FOX_HEADS), lambda b, t: (b, t, 0)),
                  _resident((1, FOX_HEADS)), _resident(place_q.shape), _resident(place_k.shape),
                  _resident(const_q.shape), _resident(const_k.shape)],
        out_specs=[spec, spec],
        out_shape=[jax.ShapeDtypeStruct((bsz, tk, LANES), BF16)] * 2,
        scratch_shapes=[pltpu.VMEM((1, FOX_HEADS), F32)],
        compiler_params=_cparams("parallel", "arbitrary"),
        name="fox_gate",
    )(lf_all, jnp.broadcast_to(shift.astype(F32), (1, FOX_HEADS)),
      jnp.asarray(place_q, BF16), jnp.asarray(place_k, BF16),
      jnp.asarray(const_q), jnp.asarray(const_k))


_FOX_MAX_BOUND = 40.0


def _fox_score_bound(q_gain, k_gain, cache_lf):
    gq = jnp.max(jnp.abs(q_gain))
    gk = jnp.max(jnp.abs(k_gain))
    bound = (math.sqrt(FOX_DH) * 1.02 * LOG2E) * gq * gk
    ok = bound <= _FOX_MAX_BOUND
    if cache_lf.size:
        ok = ok & (jnp.max(cache_lf) <= 0.0)
    return jnp.where(ok, bound, 0.0), ok


def _fox_query_operands(q_ref, qa_ref, blk):
    pair = pl.program_id(1)
    lane = lax.broadcasted_iota(jnp.int32, (blk, LANES), 1)
    low = lane < FOX_DH
    q = q_ref[0]
    qa = qa_ref[0]
    qops = []
    for s in range(2):
        lo = (2 * pair + s) * _AUG
        qh = jnp.where(low if s == 0 else ~low, q, jnp.zeros_like(q))
        qah = jnp.where((lane >= lo) & (lane < lo + _AUG), qa, jnp.zeros_like(qa))
        qops.append(jnp.concatenate([qh, qah], axis=1))
    return qops, low


def _fox_key_blocks(step, carry, blk, pblk, past, nq):
    if past:
        carry = lax.fori_loop(
            0, past // pblk,
            lambda j, c: step(pl.multiple_of(j * pblk, pblk), pblk, c, False), carry)
    if nq == 1:
        return step(past, blk, carry, True)
    qi = pl.program_id(2)
    per = 2 if blk <= 512 else 1
    wide = per * blk
    nw = qi // per

    def quad(j, c):
        start = pl.multiple_of(past + j * 2 * wide, wide)
        return step(start + wide, wide, step(start, wide, c, False), False)

    carry = lax.fori_loop(0, nw // 2, quad, carry)
    carry = lax.fori_loop(
        0, nw % 2,
        lambda j, c: step(pl.multiple_of(past + (nw // 2) * 2 * wide, wide), wide, c, False), carry)
    if per == 2:
        carry = lax.fori_loop(
            0, qi % 2, lambda j, c: step(pl.multiple_of(past + (qi - 1) * blk, blk), blk, c, False),
            carry)
    return step(pl.multiple_of(past + qi * blk, blk), blk, carry, True)


def _fox_attn_online_kernel(q_ref, qa_ref, k_ref, ka_ref, v_ref, o_ref, *, blk, pblk, past, nq):
    qops, low = _fox_query_operands(q_ref, qa_ref, blk)
    causal = _tri(blk)

    def step(start, size, carry, masked):
        rows = pl.ds(start, size)
        kop = jnp.concatenate([k_ref[0, rows, :], ka_ref[0, rows, :]], axis=1)
        v = v_ref[0, rows, :]
        out = []
        for s in range(2):
            m, l, acc = carry[s]
            sc = _dot_nt(qops[s], kop)
            if masked:
                sc = jnp.where(causal, sc, -jnp.inf)
            m_new = jnp.maximum(m, jnp.max(sc, axis=-1, keepdims=True))
            alpha = jnp.exp2(m - m_new)
            p = jnp.exp2(sc - m_new)
            l = alpha * l + jnp.sum(p, axis=-1, keepdims=True)
            acc = alpha * acc + _dot(p.astype(BF16), v)
            out.append((m_new, l, acc))
        return tuple(out)

    init = tuple((jnp.full((blk, 1), -jnp.inf, F32), jnp.zeros((blk, 1), F32),
                  jnp.zeros((blk, LANES), F32)) for _ in range(2))
    (_, l0, a0), (_, l1, a1) = _fox_key_blocks(step, init, blk, pblk, past, nq)
    o_ref[0] = jnp.where(low, a0 / l0, a1 / l1).astype(BF16)


def _fox_attn_bounded_kernel(q_ref, qa_ref, k_ref, ka_ref, vt_ref, o_ref, qop_ref, acc_ref, *,
                             blk, pblk, past, nq):
    pair = pl.program_id(1)
    lane = lax.broadcasted_iota(jnp.int32, (blk, LANES), 1)
    q = q_ref[0]
    qa = qa_ref[0]
    for s in range(2):
        lo = (2 * pair + s) * _AUG
        head_lanes = (lane < FOX_DH) if s == 0 else (lane >= FOX_DH)
        qop_ref[s * blk:(s + 1) * blk, 0:LANES] = jnp.where(head_lanes, q, jnp.zeros_like(q))
        qop_ref[s * blk:(s + 1) * blk, LANES:2 * LANES] = jnp.where(
            (lane >= lo) & (lane < lo + _AUG), qa, jnp.zeros_like(qa))
    acc_ref[...] = jnp.zeros_like(acc_ref)
    r = lax.broadcasted_iota(jnp.int32, (blk, blk), 0)
    c = lax.broadcasted_iota(jnp.int32, (blk, blk), 1)
    causal = r <= c
    chan = lax.broadcasted_iota(jnp.int32, (LANES, 1), 0) < FOX_DH

    def accumulate(start, size, qop, cols, mask):
        rows = pl.ds(start, size)
        kop = jnp.concatenate([k_ref[0, rows, :], ka_ref[0, rows, :]], axis=1)
        vt = vt_ref[0, :, rows]
        ones = jnp.ones_like(vt)
        sc = _dot_nt(kop, qop)
        ncols = sc.shape[1] // 2
        s0, s1 = sc[:, :ncols], sc[:, ncols:]
        if mask is not None:
            s0 = jnp.where(mask, s0, -jnp.inf)
            s1 = jnp.where(mask, s1, -jnp.inf)
        acc_ref[0, :, cols] += _dot(jnp.where(chan, vt, ones), jnp.exp2(s0).astype(BF16))
        acc_ref[1, :, cols] += _dot(jnp.where(chan, ones, vt), jnp.exp2(s1).astype(BF16))

    half = blk // 2

    def step(start, size, carry, masked):
        if not masked:
            accumulate(start, size, qop_ref[...], slice(None), None)
        elif half % LANES:
            accumulate(start, size, qop_ref[...], slice(None), causal)
        else:
            accumulate(start, half, qop_ref[...], slice(None), causal[:half])
            late = jnp.concatenate([qop_ref[half:blk], qop_ref[blk + half:2 * blk]], axis=0)
            accumulate(start + half, half, late, slice(half, blk), causal[:half, :half])
        return carry

    _fox_key_blocks(step, 0, blk, pblk, past, nq)
    a0 = acc_ref[0]
    a1 = acc_ref[1]
    out_t = jnp.concatenate([a0[:FOX_DH] / a0[FOX_DH:], a1[FOX_DH:] / a1[:FOX_DH]], axis=0)
    o_ref[0] = out_t.T.astype(BF16)


def _fox_attn(body, transposed_v, qs, qa, kb, ka, v, bsz, seq, past, blk, pblk):
    scratch = ([pltpu.VMEM((2 * blk, 2 * LANES), BF16), pltpu.VMEM((2, LANES, blk), F32)]
               if transposed_v else [])
    tk = past + seq
    d = qs.shape[-1]
    pb = past // blk
    qspec = pl.BlockSpec((1, blk, LANES), lambda b, p, i: (b, i, p))
    kspec = pl.BlockSpec((1, tk, LANES), lambda b, p, i: (b, 0, p))
    vspec = pl.BlockSpec((1, LANES, tk), lambda b, p, i: (b, p, 0)) if transposed_v else kspec
    return pl.pallas_call(
        functools.partial(body, blk=blk, pblk=pblk, past=past, nq=seq // blk),
        grid=(bsz, d // LANES, seq // blk),
        in_specs=[qspec,
                  pl.BlockSpec((1, blk, LANES), lambda b, p, i: (b, pb + i, 0)),
                  kspec,
                  pl.BlockSpec((1, tk, LANES), lambda b, p, i: (b, 0, 0)),
                  vspec],
        out_specs=qspec,
        out_shape=jax.ShapeDtypeStruct((bsz, seq, d), BF16),
        scratch_shapes=scratch,
        compiler_params=_cparams("parallel", "parallel", "arbitrary"),
        name=body.__name__.strip("_").replace("_kernel", ""),
    )(qs, qa, kb, ka, v)


def _trunk(x, pos0, hg_s, ret_s, conv_s, fox_k, fox_v, fox_lf, W):
    bsz, seq, d = x.shape
    m = bsz * seq
    past = fox_k.shape[1]
    tm = min(seq, 256)
    tm_proj = min(seq, 512)
    tm_ffn = min(m, 512)
    x = x.reshape(m, d)

    def ffn(x, a, w_out, layer):
        return _outproj_ffn(x, a, w_out, W["norm_ffn"][layer], W["ffn_w_g"][layer],
                            W["ffn_w_u"][layer], W["ffn_w_d"][layer], tm_ffn, FFN_BOUNDS)

    proj = _norm_proj(x, W["norm_mix"][0], W["hg_w_in"], tm_proj, 512)
    a, hg_s = _hgrn_core(proj, W["hg_lower_bounds"], W["hg_out_gain"], hg_s, 0, bsz, seq, tm)
    x = ffn(x, a, W["hg_w_out"], 0)

    inv = jnp.power(ROPE_BASE, -jnp.arange(0, RET_DK, 2, dtype=F32) / RET_DK)
    cos, sin = _rope_table(inv, pos0, seq, tm)
    chunk = min(seq, 256)
    parts = _ret_proj(x, W["norm_mix"][1], W["ret_w_in"], cos, sin, seq, tm_proj, chunk)
    a, ret_s = _ret_core(*parts, W["ret_gn_gain"], ret_s, bsz, seq, tm, chunk)
    x = ffn(x, a, W["ret_w_out"], 1)

    a, conv_s = _conv_core(x, W["norm_mix"][2], W["conv_w_in"], W["conv_w"], conv_s, bsz, seq,
                            tm_proj)
    x = ffn(x, a, W["conv_w_out"], 2)

    qs, kb, vb, vt, lf, kn, v = _fox_prep(x, W["norm_mix"][3], W["fox_w_qkv"], W["fox_w_f"],
                                          W["fox_b_f"], W["fox_q_gain"], W["fox_k_gain"], bsz,
                                          tm_proj)
    blk = min(seq, 1024)
    lf3 = lf.reshape(bsz, seq, FOX_HEADS)
    kb3 = kb.reshape(bsz, seq, d)
    vb3 = vb.reshape(bsz, seq, d)
    if past:
        lf3 = jnp.concatenate([fox_lf, lf3], axis=1)
        kb3 = jnp.concatenate([fox_k.reshape(bsz, past, d).astype(BF16), kb3], axis=1)
        past_v = fox_v.reshape(bsz, past, d).astype(BF16)
        vb3 = jnp.concatenate([past_v, vb3], axis=1)
        vt = jnp.concatenate([past_v.transpose(0, 2, 1), vt], axis=2)
    shift, bounded = _fox_score_bound(W["fox_q_gain"], W["fox_k_gain"], fox_lf)
    qa, ka = _fox_gate(lf3, shift)
    attn = functools.partial(_fox_attn, bsz=bsz, seq=seq, past=past, blk=blk,
                             pblk=min(past, 512) if past else blk)
    q3 = qs.reshape(bsz, seq, d)
    a = lax.cond(bounded,
                 lambda: attn(_fox_attn_bounded_kernel, True, q3, qa, kb3, ka, vt),
                 lambda: attn(_fox_attn_online_kernel, False, q3, qa, kb3, ka, vb3))
    x = ffn(x, a.reshape(m, d), W["fox_w_out"], 3)

    return (x.reshape(bsz, seq, d), hg_s, ret_s, conv_s,
            kn.reshape(bsz, seq, FOX_HEADS, FOX_DH), v.reshape(bsz, seq, FOX_HEADS, FOX_DH),
            lf.reshape(bsz, seq, FOX_HEADS))


def kernel(x_prompt, x_sample, state_hgrn, state_ret, state_conv, cache_fox_k, cache_fox_v, cache_fox_logf, norm_mix, norm_ffn, hg_w_in, hg_lower_bounds, hg_out_gain, hg_w_out, ret_w_in, ret_gn_gain, ret_w_out, conv_w_in, conv_w, conv_w_out, fox_w_in, fox_b_f, fox_q_gain, fox_k_gain, fox_w_out, ffn_w_gu, ffn_w_down):
    d = D_MODEL
    bf = lambda w: w.astype(BF16)
    W = dict(
        norm_mix=norm_mix, norm_ffn=norm_ffn,
        hg_w_in=bf(hg_w_in), hg_lower_bounds=hg_lower_bounds, hg_out_gain=hg_out_gain,
        hg_w_out=bf(hg_w_out),
        ret_w_in=bf(ret_w_in), ret_gn_gain=ret_gn_gain, ret_w_out=bf(ret_w_out),
        conv_w_in=bf(conv_w_in), conv_w=conv_w, conv_w_out=bf(conv_w_out),
        fox_w_qkv=bf(fox_w_in[:, :3 * d]),
        fox_w_f=bf(jnp.pad(fox_w_in[:, 3 * d:], ((0, 0), (0, LANES - FOX_HEADS)))),
        fox_b_f=jnp.pad(fox_b_f, (0, LANES - FOX_HEADS)).reshape(1, LANES),
        fox_q_gain=fox_q_gain, fox_k_gain=fox_k_gain, fox_w_out=bf(fox_w_out),
        ffn_w_g=bf(ffn_w_gu[:, :, :D_FF]), ffn_w_u=bf(ffn_w_gu[:, :, D_FF:]), ffn_w_d=bf(ffn_w_down),
    )
    bsz = x_prompt.shape[0]
    dt = x_prompt.dtype
    (y_p, hg_p, ret_p, conv_p, fk_p, fv_p, flf_p) = _trunk(
        x_prompt, 0,
        jnp.zeros((bsz, HG_HEADS, HG_DK, HG_DV), F32),
        jnp.zeros((bsz, RET_HEADS, RET_DK, RET_DV), F32),
        jnp.zeros((bsz, CONV_WIDTH - 1, d), dt),
        jnp.zeros((bsz, 0, FOX_HEADS, FOX_DH), dt),
        jnp.zeros((bsz, 0, FOX_HEADS, FOX_DH), dt),
        jnp.zeros((bsz, 0, FOX_HEADS), F32), W)
    past = cache_fox_k.shape[1]
    (y_s, hg_s, ret_s, conv_s, fk_s, fv_s, flf_s) = _trunk(
        x_sample, past, state_hgrn, state_ret, state_conv,
        cache_fox_k, cache_fox_v, cache_fox_logf, W)
    return (y_p, y_s, hg_p, hg_s, ret_p, ret_s, conv_p, conv_s,
            fk_p, fv_p, flf_p, fk_s, fv_s, flf_s)
```
